```python
import math
import jax, jax.numpy as jnp
from jax import lax
import numpy as np

D_MODEL = 1024
BATCH = 8
SEQ = 4096
DEPTH = 2

GRID_W = 64
D_MIX = D_MODEL
ATTN_WIDTH = D_MIX // 2
SSD_WIDTH = D_MIX - ATTN_WIDTH
HEAD_DIM = 64
N_Q_HEADS = ATTN_WIDTH // HEAD_DIM
N_KV_HEADS = 2
KV_WIDTH = N_KV_HEADS * HEAD_DIM
Q_BLOCK = 128
ROPE_THETA = 10000.0
ROPE_AXIS_DIM = HEAD_DIM // 2
SSD_HEAD_DIM = 64
SSD_HEADS = SSD_WIDTH // SSD_HEAD_DIM
SSD_GROUPS = 2
D_STATE = 128
D_CONV = 5
CONV_PAD = D_CONV // 2
CHUNK = 128
CONV_CH = SSD_WIDTH + 2 * SSD_GROUPS * D_STATE
DT_MIN = 0.001
DT_MAX = 0.1
D_IN_PROJ = ATTN_WIDTH + 2 * KV_WIDTH + SSD_WIDTH + CONV_CH + 2 * SSD_HEADS
D_FF = -(-8 * D_MODEL // (3 * 256)) * 256
EPS = 1e-6

kernel_name = "hybrid_ssd_gqa_axial_rope_encoder"


def rms_norm(x, w):
    xf = x.astype(jnp.float32)
    y = xf * lax.rsqrt(jnp.mean(xf * xf, axis=-1, keepdims=True) + EPS)
    return (y * w.astype(jnp.float32)).astype(x.dtype)


def _rot_half(x, cos, sin):
    x1, x2 = jnp.split(x, 2, axis=-1)
    return jnp.concatenate([x1 * cos - x2 * sin, x2 * cos + x1 * sin], axis=-1)


def axial_rope(x, cos_r, sin_r, cos_c, sin_c):
    xr, xc = jnp.split(x, 2, axis=-1)
    return jnp.concatenate([_rot_half(xr, cos_r, sin_r), _rot_half(xc, cos_c, sin_c)], axis=-1)


def axial_rope_tables(seq_len):
    rows = seq_len // GRID_W
    row_idx, col_idx = jnp.meshgrid(jnp.arange(rows), jnp.arange(GRID_W), indexing="ij")
    row_idx = row_idx.reshape(-1).astype(jnp.float32)
    col_idx = col_idx.reshape(-1).astype(jnp.float32)
    half = ROPE_AXIS_DIM // 2
    inv_freq = ROPE_THETA ** (-(2.0 * jnp.arange(half, dtype=jnp.float32)) / ROPE_AXIS_DIM)
    ang_r = row_idx[:, None] * inv_freq[None, :]
    ang_c = col_idx[:, None] * inv_freq[None, :]
    return (jnp.cos(ang_r)[:, None], jnp.sin(ang_r)[:, None],
            jnp.cos(ang_c)[:, None], jnp.sin(ang_c)[:, None])


def blocked_gqa(q, k, v):
    b, l, hq, d = q.shape
    r = hq // N_KV_HEADS
    nb = l // Q_BLOCK
    qb = (q * (1.0 / math.sqrt(d))).reshape(b, nb, Q_BLOCK, N_KV_HEADS, r, d)
    qb = jnp.moveaxis(qb, 1, 0)

    def one_block(qi):
        s = jnp.einsum("bqgrd,bkgd->bgrqk", qi, k)
        p = jax.nn.softmax(s, axis=-1)
        return jnp.einsum("bgrqk,bkgd->bqgrd", p, v)

    out = lax.map(one_block, qb)
    return jnp.moveaxis(out, 0, 1).reshape(b, l, hq * d)


def depthwise_conv_centred(x, w, bias):
    y = lax.conv_general_dilated(
        x, w[:, None, :].astype(x.dtype), window_strides=(1,), padding=[(CONV_PAD, CONV_PAD)],
        dimension_numbers=("NWC", "WIO", "NWC"), feature_group_count=x.shape[-1])
    return y + bias.astype(x.dtype)


def ssd_chunked(x, dt, a, bm, cm):
    b, l, h, p = x.shape
    g, n = bm.shape[2], bm.shape[3]
    r = h // g
    nc = l // CHUNK
    xc = (x * dt[..., None]).reshape(b, nc, CHUNK, g, r, p)
    ac = (dt * a).reshape(b, nc, CHUNK, g, r)
    bc = bm.reshape(b, nc, CHUNK, g, n)
    cc = cm.reshape(b, nc, CHUNK, g, n)
    a_cs = jnp.cumsum(ac, axis=2)
    diff = a_cs[:, :, :, None] - a_cs[:, :, None, :]
    mask = jnp.tril(jnp.ones((CHUNK, CHUNK), dtype=bool))[:, :, None, None]
    decay = jnp.exp(jnp.where(mask, diff, -jnp.inf))
    scores = jnp.einsum("bclgn,bcsgn->bclsg", cc, bc)
    y_diag = jnp.einsum("bclsgr,bcsgrp->bclgrp", scores[..., None] * decay, xc)
    decay_to_end = jnp.exp(a_cs[:, :, -1:] - a_cs)
    states = jnp.einsum("bclgn,bclgrp->bcgrpn", bc, xc * decay_to_end[..., None])
    chunk_decay = jnp.exp(a_cs[:, :, -1])

    def step(hs, inp):
        dec, st = inp
        return hs * dec[..., None, None] + st, hs

    h0 = jnp.zeros((b, g, r, p, n), dtype=states.dtype)
    _, states_in = lax.scan(step, h0, (jnp.moveaxis(chunk_decay, 1, 0), jnp.moveaxis(states, 1, 0)))
    states_in = jnp.moveaxis(states_in, 0, 1)
    y_off = jnp.einsum("bclgn,bcgrpn->bclgrp", cc, states_in) * jnp.exp(a_cs)[..., None]
    return (y_diag + y_off).reshape(b, l, h, p)


def ssd_mixer(z, xbc_raw, dt_raw, conv_w, conv_b, dt_bias, a_log, d_skip, norm_w):
    b, l, _ = z.shape
    xbc = jax.nn.silu(depthwise_conv_centred(xbc_raw, conv_w, conv_b)).astype(jnp.float32)
    xs, bm, cm = jnp.split(xbc, [SSD_WIDTH, SSD_WIDTH + SSD_GROUPS * D_STATE], axis=-1)
    xs = xs.reshape(b, l, SSD_HEADS, SSD_HEAD_DIM)
    bm = bm.reshape(b, l, SSD_GROUPS, D_STATE)
    cm = cm.reshape(b, l, SSD_GROUPS, D_STATE)
    dt = jax.nn.softplus(dt_raw.astype(jnp.float32).reshape(b, l, 2, SSD_HEADS)
                         + dt_bias.astype(jnp.float32))
    a = -jnp.exp(a_log.astype(jnp.float32))
    y_fwd = ssd_chunked(xs, dt[:, :, 0], a[0], bm, cm)
    flip = lambda t: jnp.flip(t, axis=1)
    y_bwd = flip(ssd_chunked(flip(xs), flip(dt[:, :, 1]), a[1], flip(bm), flip(cm)))
    y = y_fwd + y_bwd + xs * d_skip.astype(jnp.float32)[:, None]
    y = y.reshape(b, l, SSD_WIDTH) * jax.nn.silu(z.astype(jnp.float32))
    yg = y.reshape(b, l, SSD_GROUPS, SSD_WIDTH // SSD_GROUPS)
    yg = yg * lax.rsqrt(jnp.mean(yg * yg, axis=-1, keepdims=True) + EPS)
    return (yg.reshape(b, l, SSD_WIDTH) * norm_w.astype(jnp.float32)).astype(z.dtype)


def setup_inputs(seed: int = 0) -> dict:
    key = jax.random.key(seed)
    ks = jax.random.split(key, 20)
    f32 = jnp.float32
    nrm = lambda k, shape, s: jax.random.normal(k, shape, f32) * s
    dt0 = jnp.exp(jax.random.uniform(ks[7], (DEPTH, 2, SSD_HEADS), f32,
                                     math.log(DT_MIN), math.log(DT_MAX)))
    return {
        "x": jax.random.normal(ks[0], (BATCH, SEQ, D_MODEL), f32),
        "norm_mix_w": 1.0 + nrm(ks[1], (DEPTH, D_MODEL), 0.02),
        "w_in": nrm(ks[2], (DEPTH, D_MODEL, D_IN_PROJ), D_MODEL ** -0.5),
        "q_norm_w": 1.0 + nrm(ks[3], (DEPTH, HEAD_DIM), 0.02),
        "k_norm_w": 1.0 + nrm(ks[4], (DEPTH, HEAD_DIM), 0.02),
        "conv_w": nrm(ks[5], (DEPTH, D_CONV, CONV_CH), D_CONV ** -0.5),
        "conv_b": nrm(ks[6], (DEPTH, CONV_CH), 0.01),
        "dt_bias": dt0 + jnp.log(-jnp.expm1(-dt0)),
        "a_log": jnp.log(jax.random.uniform(ks[8], (DEPTH, 2, SSD_HEADS), f32, 1.0, 16.0)),
        "d_skip": 1.0 + nrm(ks[9], (DEPTH, SSD_HEADS), 0.1),
        "ssd_norm_w": 1.0 + nrm(ks[10], (DEPTH, SSD_WIDTH), 0.02),
        "w_out": nrm(ks[11], (DEPTH, D_MIX, D_MODEL), D_MIX ** -0.5),
        "norm_ffn_w": 1.0 + nrm(ks[12], (DEPTH, D_MODEL), 0.02),
        "w_gate": nrm(ks[13], (DEPTH, D_MODEL, D_FF), D_MODEL ** -0.5),
        "w_up": nrm(ks[14], (DEPTH, D_MODEL, D_FF), D_MODEL ** -0.5),
        "w_down": nrm(ks[15], (DEPTH, D_FF, D_MODEL), D_FF ** -0.5),
        "final_norm_w": 1.0 + nrm(ks[16], (D_MODEL,), 0.02),
    }


def reference(x, norm_mix_w, w_in, q_norm_w, k_norm_w, conv_w, conv_b, dt_bias, a_log, d_skip,
              ssd_norm_w, w_out, norm_ffn_w, w_gate, w_up, w_down, final_norm_w):
    b, l, _ = x.shape
    cos_r, sin_r, cos_c, sin_c = axial_rope_tables(l)
    split_at = np.cumsum([ATTN_WIDTH, KV_WIDTH, KV_WIDTH, SSD_WIDTH, CONV_CH]).tolist()
    for i in range(DEPTH):
        h = rms_norm(x, norm_mix_w[i])
        proj = jnp.einsum("bld,de->ble", h, w_in[i])
        q, k, v, z, xbc_raw, dt_raw = jnp.split(proj, split_at, axis=-1)
        q = rms_norm(q.reshape(b, l, N_Q_HEADS, HEAD_DIM), q_norm_w[i]).astype(jnp.float32)
        k = rms_norm(k.reshape(b, l, N_KV_HEADS, HEAD_DIM), k_norm_w[i]).astype(jnp.float32)
        v = v.reshape(b, l, N_KV_HEADS, HEAD_DIM).astype(jnp.float32)
        q = axial_rope(q, cos_r, sin_r, cos_c, sin_c)
        k = axial_rope(k, cos_r, sin_r, cos_c, sin_c)
        attn_out = blocked_gqa(q, k, v).astype(x.dtype)
        ssd_out = ssd_mixer(z, xbc_raw, dt_raw, conv_w[i], conv_b[i], dt_bias[i], a_log[i],
                            d_skip[i], ssd_norm_w[i])
        mixed = jnp.concatenate([attn_out, ssd_out], axis=-1)
        x = x + jnp.einsum("ble,ed->bld", mixed, w_out[i])
        h = rms_norm(x, norm_ffn_w[i])
        g = jnp.einsum("bld,df->blf", h, w_gate[i])
        u = jnp.einsum("bld,df->blf", h, w_up[i])
        x = x + jnp.einsum("blf,fd->bld", jax.nn.silu(g) * u, w_down[i])
    return rms_norm(x, final_norm_w)
```

```python
import functools
import math

import jax
import jax.numpy as jnp
from jax import lax
from jax.experimental import pallas as pl
from jax.experimental.pallas import tpu as pltpu

F32 = jnp.float32
BF16 = jnp.bfloat16

HEAD_DIM = 64
N_Q_HEADS = 8
N_KV_HEADS = 2
ATTN_WIDTH = N_Q_HEADS * HEAD_DIM
KV_WIDTH = N_KV_HEADS * HEAD_DIM
QK_WIDTH = ATTN_WIDTH + KV_WIDTH
SSD_HEADS = 8
SSD_HEAD_DIM = 64
SSD_WIDTH = SSD_HEADS * SSD_HEAD_DIM
SSD_GROUPS = 2
HEADS_PER_GROUP = SSD_HEADS // SSD_GROUPS
GROUP_WIDTH = SSD_WIDTH // SSD_GROUPS
D_STATE = 128
D_CONV = 5
CONV_PAD = D_CONV // 2
CONV_CH = SSD_WIDTH + 2 * SSD_GROUPS * D_STATE
GRID_W = 64
ROPE_THETA = 10000.0
ROPE_AXIS_DIM = HEAD_DIM // 2
EPS = 1e-6

LANES = 128
SUBLANES = 8
VMEM_LIMIT = 56 * 1024 * 1024

OFF_QK = 0
OFF_V = QK_WIDTH
OFF_Z = OFF_V + KV_WIDTH
OFF_XBC = OFF_Z + SSD_WIDTH
OFF_DT = OFF_XBC + CONV_CH
D_IN_PAD = OFF_DT + LANES

TM_MIX = 512
TQ_ATTN = 256
TC_CONV = 512
SSD_CHUNK = 128
TS_SSD = 512
TM_FFN = 512
FF_SPLIT = 1536


def _sigmoid(x):
    return 1.0 / (1.0 + jnp.exp(-x))


def _split3(a):
    hi = a.astype(BF16)
    r1 = a - hi.astype(F32)
    mid = r1.astype(BF16)
    lo = (r1 - mid.astype(F32)).astype(BF16)
    return hi, mid, lo


def _dot(a, b):
    return jnp.dot(a, b, preferred_element_type=F32)


def _dot_exact(a, m01):
    hi, mid, lo = _split3(a)
    return _dot(hi, m01) + _dot(mid, m01) + _dot(lo, m01)


def _mix_in_kernel(x_ref, nw_ref, w_ref, qkw_ref, cos_ref, sin_ref, dtb_ref,
                   q_ref, kt_ref, v_ref, z_ref, xbc_ref, dt_ref):
    x = x_ref[0]
    ms = jnp.mean(x * x, axis=-1, keepdims=True)
    h = (x * lax.rsqrt(ms + EPS) * nw_ref[...]).astype(BF16)
    proj = _dot(h, w_ref[...])

    qk = proj[:, OFF_QK:OFF_QK + QK_WIDTH]
    n_heads = QK_WIDTH // HEAD_DIM
    ch = lax.broadcasted_iota(jnp.int32, (QK_WIDTH, LANES), 0) // HEAD_DIM
    hd = lax.broadcasted_iota(jnp.int32, (QK_WIDTH, LANES), 1)
    ind = jnp.where(ch == hd, 1.0, 0.0).astype(BF16)
    hd_t = lax.broadcasted_iota(jnp.int32, (LANES, QK_WIDTH), 0)
    ch_t = lax.broadcasted_iota(jnp.int32, (LANES, QK_WIDTH), 1) // HEAD_DIM
    ind_t = jnp.where(ch_t == hd_t, 1.0, 0.0).astype(BF16)
    ss = _dot_exact(qk * qk, ind)
    inv = lax.rsqrt(ss * (1.0 / HEAD_DIM) + EPS)
    inv_full = _dot_exact(inv, ind_t)
    qkn = qk * inv_full * qkw_ref[...]

    cos = cos_ref[...]
    sin = sin_ref[...]
    lane = lax.broadcasted_iota(jnp.int32, cos.shape, 1)
    first_half = (lane & (ROPE_AXIS_DIM // 2)) == 0
    rot = []
    for c in range(QK_WIDTH // LANES):
        xc = qkn[:, c * LANES:(c + 1) * LANES]
        partner = jnp.where(first_half,
                            pltpu.roll(xc, LANES - ROPE_AXIS_DIM // 2, 1),
                            pltpu.roll(xc, ROPE_AXIS_DIM // 2, 1))
        rot.append(xc * cos + partner * sin)

    for c in range(ATTN_WIDTH // LANES):
        q_ref[0, 2 * c] = rot[c][:, :HEAD_DIM].astype(BF16)
        q_ref[0, 2 * c + 1] = rot[c][:, HEAD_DIM:].astype(BF16)
    kt = rot[ATTN_WIDTH // LANES].T
    kt_ref[0, 0, 0] = kt[:HEAD_DIM].astype(BF16)
    kt_ref[0, 1, 0] = kt[HEAD_DIM:].astype(BF16)

    vv = proj[:, OFF_V:OFF_V + KV_WIDTH]
    ones_col = jnp.where(lane == HEAD_DIM, 1.0, 0.0)
    v_ref[0, 0] = jnp.where(lane < HEAD_DIM, vv, ones_col).astype(BF16)
    v_ref[0, 1] = jnp.where(lane < HEAD_DIM, pltpu.roll(vv, HEAD_DIM, 1), ones_col).astype(BF16)

    z_ref[0] = proj[:, OFF_Z:OFF_Z + SSD_WIDTH]
    xbc_ref[0] = proj[:, OFF_XBC:OFF_XBC + CONV_CH]
    dtp = proj[:, OFF_DT:OFF_DT + LANES] + dtb_ref[...]
    dt_ref[0] = jnp.maximum(dtp, 0.0) + jnp.log1p(jnp.exp(-jnp.abs(dtp)))


def _mix_in(x, nw, w_in_pad, qkw, cos, sin, dtb):
    b, l, d = x.shape
    tm = TM_MIX
    nt = l // tm
    const = lambda shape: pl.BlockSpec(shape, lambda bi, i: (0,) * len(shape))
    return pl.pallas_call(
        _mix_in_kernel,
        grid=(b, nt),
        in_specs=[
            pl.BlockSpec((1, tm, d), lambda bi, i: (bi, i, 0)),
            const((1, d)),
            const((d, D_IN_PAD)),
            const((1, QK_WIDTH)),
            pl.BlockSpec((tm, LANES), lambda bi, i: (i, 0)),
            pl.BlockSpec((tm, LANES), lambda bi, i: (i, 0)),
            const((1, LANES)),
        ],
        out_specs=[
            pl.BlockSpec((1, N_Q_HEADS, tm, HEAD_DIM), lambda bi, i: (bi, 0, i, 0)),
            pl.BlockSpec((1, N_KV_HEADS, 1, HEAD_DIM, tm), lambda bi, i: (bi, 0, i, 0, 0)),
            pl.BlockSpec((1, N_KV_HEADS, tm, LANES), lambda bi, i: (bi, 0, i, 0)),
            pl.BlockSpec((1, tm, SSD_WIDTH), lambda bi, i: (bi, i, 0)),
            pl.BlockSpec((1, tm, CONV_CH), lambda bi, i: (bi, i, 0)),
            pl.BlockSpec((1, tm, LANES), lambda bi, i: (bi, i, 0)),
        ],
        out_shape=[
            jax.ShapeDtypeStruct((b, N_Q_HEADS, l, HEAD_DIM), BF16),
            jax.ShapeDtypeStruct((b, N_KV_HEADS, nt, HEAD_DIM, tm), BF16),
            jax.ShapeDtypeStruct((b, N_KV_HEADS, l, LANES), BF16),
            jax.ShapeDtypeStruct((b, l, SSD_WIDTH), F32),
            jax.ShapeDtypeStruct((b, l, CONV_CH), F32),
            jax.ShapeDtypeStruct((b, l, LANES), F32),
        ],
        compiler_params=pltpu.CompilerParams(
            dimension_semantics=("parallel", "parallel"), vmem_limit_bytes=VMEM_LIMIT),
        name="mix_in",
    )(x, nw, w_in_pad, qkw, cos, sin, dtb)


def _attn_kernel(q_ref, kt_ref, v_ref, o_ref, m_sc, acc_sc, *, nk, tk, tq):
    q = q_ref[0].reshape(2 * tq, HEAD_DIM)
    m_sc[...] = jnp.full(m_sc.shape, -jnp.inf, F32)
    acc_sc[...] = jnp.zeros(acc_sc.shape, F32)

    def body(c, carry):
        s = _dot(q, kt_ref[0, 0, c])
        m_prev = m_sc[...]
        m_new = jnp.maximum(m_prev, jnp.max(s, axis=-1, keepdims=True))
        p = jnp.exp(s - m_new).astype(BF16)
        vv = v_ref[0, 0, pl.ds(pl.multiple_of(c * tk, tk), tk), :]
        acc_sc[...] = jnp.exp(m_prev - m_new) * acc_sc[...] + _dot(p, vv)
        m_sc[...] = m_new
        return carry

    lax.fori_loop(0, nk, body, 0)
    acc = acc_sc[...]
    o = acc * (1.0 / acc[:, HEAD_DIM:HEAD_DIM + 1])
    lane = lax.broadcasted_iota(jnp.int32, (tq, LANES), 1)
    o_ref[0] = jnp.where(lane < HEAD_DIM, o[:tq], pltpu.roll(o[tq:], HEAD_DIM, 1)).astype(BF16)


def _attention(q, kt, v):
    b, _, l, _ = q.shape
    nk, tk = kt.shape[2], kt.shape[4]
    tq = TQ_ATTN
    pairs_per_kv = N_Q_HEADS // N_KV_HEADS // 2
    return pl.pallas_call(
        functools.partial(_attn_kernel, nk=nk, tk=tk, tq=tq),
        grid=(b, N_Q_HEADS // 2, l // tq),
        in_specs=[
            pl.BlockSpec((1, 2, tq, HEAD_DIM), lambda bi, hp, i: (bi, hp, i, 0)),
            pl.BlockSpec((1, 1, nk, HEAD_DIM, tk), lambda bi, hp, i: (bi, hp // pairs_per_kv, 0, 0, 0)),
            pl.BlockSpec((1, 1, l, LANES), lambda bi, hp, i: (bi, hp // pairs_per_kv, 0, 0)),
        ],
        out_specs=pl.BlockSpec((1, tq, LANES), lambda bi, hp, i: (bi, i, hp)),
        out_shape=jax.ShapeDtypeStruct((b, l, ATTN_WIDTH), BF16),
        scratch_shapes=[pltpu.VMEM((2 * tq, 1), F32), pltpu.VMEM((2 * tq, LANES), F32)],
        compiler_params=pltpu.CompilerParams(
            dimension_semantics=("parallel", "parallel", "parallel"), vmem_limit_bytes=VMEM_LIMIT),
        name="attn",
    )(q, kt, v)


def _conv_kernel(prev_ref, cur_ref, next_ref, dt_ref, w_ref, b_ref, xt_ref, dtt_ref, ext_sc, *, tc):
    i = pl.program_id(1)
    has_prev = (i > 0).astype(F32)
    has_next = (i < pl.num_programs(1) - 1).astype(F32)
    ext_sc[0:SUBLANES] = prev_ref[0] * has_prev
    ext_sc[SUBLANES:SUBLANES + tc] = cur_ref[0]
    ext_sc[SUBLANES + tc:2 * SUBLANES + tc] = next_ref[0] * has_next
    w = w_ref[...]
    acc = jnp.broadcast_to(b_ref[...], (tc, CONV_CH))
    for k in range(D_CONV):
        off = SUBLANES - CONV_PAD + k
        acc = acc + ext_sc[off:off + tc] * w[k:k + 1]
    y = acc * _sigmoid(acc)
    xt_ref[0] = y.T
    dtt_ref[0] = dt_ref[0].T[:2 * SSD_HEADS]


def _conv(xbc_raw, dt, conv_w, conv_b):
    b, l, c = xbc_raw.shape
    tc = TC_CONV
    rb = tc // SUBLANES
    last = l // SUBLANES - 1
    return pl.pallas_call(
        functools.partial(_conv_kernel, tc=tc),
        grid=(b, l // tc),
        in_specs=[
            pl.BlockSpec((1, SUBLANES, c), lambda bi, i: (bi, jnp.maximum(i * rb - 1, 0), 0)),
            pl.BlockSpec((1, tc, c), lambda bi, i: (bi, i, 0)),
            pl.BlockSpec((1, SUBLANES, c), lambda bi, i: (bi, jnp.minimum((i + 1) * rb, last), 0)),
            pl.BlockSpec((1, tc, LANES), lambda bi, i: (bi, i, 0)),
            pl.BlockSpec((D_CONV, c), lambda bi, i: (0, 0)),
            pl.BlockSpec((1, c), lambda bi, i: (0, 0)),
        ],
        out_specs=[
            pl.BlockSpec((1, c, tc), lambda bi, i: (bi, 0, i)),
            pl.BlockSpec((1, 2 * SSD_HEADS, tc), lambda bi, i: (bi, 0, i)),
        ],
        out_shape=[
            jax.ShapeDtypeStruct((b, c, l), F32),
            jax.ShapeDtypeStruct((b, 2 * SSD_HEADS, l), F32),
        ],
        scratch_shapes=[pltpu.VMEM((tc + 2 * SUBLANES, c), F32)],
        compiler_params=pltpu.CompilerParams(
            dimension_semantics=("parallel", "parallel"), vmem_limit_bytes=VMEM_LIMIT),
        name="conv",
    )(xbc_raw, xbc_raw, xbc_raw, dt, conv_w, conv_b)


def _ssd_kernel(xf_ref, dtf_ref, xb_ref, dtb_ref, alog_ref, dskip_ref, yf_ref, yb_ref,
                sf_sc, sb_sc, *, nck):
    q = SSD_CHUNK

    @pl.when(pl.program_id(1) == 0)
    def _():
        sf_sc[...] = jnp.zeros(sf_sc.shape, F32)
        sb_sc[...] = jnp.zeros(sb_sc.shape, F32)

    a16 = -jnp.exp(alog_ref[...])
    r = lax.broadcasted_iota(jnp.int32, (q, q), 0)
    c = lax.broadcasted_iota(jnp.int32, (q, q), 1)
    fwd_mask = c >= r
    bwd_mask = c <= r
    incl_fwd = jnp.where(fwd_mask, 1.0, 0.0).astype(BF16)
    incl_rev = jnp.where(bwd_mask, 1.0, 0.0).astype(BF16)
    row16 = lax.broadcasted_iota(jnp.int32, (2 * SSD_HEADS, q), 0)
    dsk = dskip_ref[...]

    def rows64(v8, h0):
        return jnp.concatenate(
            [jnp.broadcast_to(v8[h0 + k:h0 + k + 1], (SSD_HEAD_DIM, 1)) for k in range(HEADS_PER_GROUP)],
            axis=0)

    for j in range(nck):
        cols = slice(j * q, (j + 1) * q)
        dt_t = dtf_ref[0, :, cols]
        ac = dt_t * a16
        cs = (_dot_exact(jnp.where(row16 < SSD_HEADS, ac, 0.0), incl_fwd)
              + _dot_exact(jnp.where(row16 >= SSD_HEADS, ac, 0.0), incl_rev))
        cs_col = jnp.concatenate([cs, jnp.zeros((q - 2 * SSD_HEADS, q), F32)], axis=0).T
        a_last = cs[:SSD_HEADS, q - 1:q]
        exp_a = jnp.exp(cs[:SSD_HEADS])
        wf = dt_t[:SSD_HEADS] * jnp.exp(a_last - cs[:SSD_HEADS])
        dec = jnp.exp(a_last)
        ys = []
        for g in range(SSD_GROUPS):
            bt = xf_ref[0, SSD_WIDTH + g * D_STATE:SSD_WIDTH + (g + 1) * D_STATE, cols]
            ct = xf_ref[0, SSD_WIDTH + (SSD_GROUPS + g) * D_STATE:
                        SSD_WIDTH + (SSD_GROUPS + g + 1) * D_STATE, cols]
            bg = bt.T.astype(BF16)
            ctb = ct.astype(BF16)
            sc_t = _dot(bg, ctb)
            st = sf_sc[g]
            yoff = _dot(st.astype(BF16), ctb)
            xws = []
            for k in range(HEADS_PER_GROUP):
                hh = g * HEADS_PER_GROUP + k
                xh = xf_ref[0, hh * SSD_HEAD_DIM:(hh + 1) * SSD_HEAD_DIM, cols]
                arg = jnp.where(fwd_mask,
                                cs[hh:hh + 1] - cs_col[:, hh:hh + 1],
                                cs[SSD_HEADS + hh:SSD_HEADS + hh + 1]
                                - cs_col[:, SSD_HEADS + hh:SSD_HEADS + hh + 1])
                e = jnp.exp(arg)
                up = (jnp.where(fwd_mask, e, 0.0) * sc_t).astype(BF16)
                lo = (jnp.where(bwd_mask, e, 0.0) * sc_t).astype(BF16)
                lhs = jnp.concatenate([xh * dt_t[hh:hh + 1], xh * dt_t[SSD_HEADS + hh:SSD_HEADS + hh + 1]],
                                      axis=1).astype(BF16)
                yd = _dot(lhs, jnp.concatenate([up, lo], axis=0))
                ys.append(yd + yoff[k * SSD_HEAD_DIM:(k + 1) * SSD_HEAD_DIM] * exp_a[hh:hh + 1]
                          + xh * dsk[hh * SSD_HEAD_DIM:(hh + 1) * SSD_HEAD_DIM])
                xws.append(xh * wf[hh:hh + 1])
            xw = jnp.concatenate(xws, axis=0).astype(BF16)
            sf_sc[g] = st * rows64(dec, g * HEADS_PER_GROUP) + _dot(xw, bg)
        yf_ref[0, :, cols] = jnp.concatenate(ys, axis=0)

    for j in reversed(range(nck)):
        cols = slice(j * q, (j + 1) * q)
        dt_b = dtb_ref[0, SSD_HEADS:, cols]
        rcs = _dot_exact(dt_b * a16[SSD_HEADS:], incl_rev)
        r_first = rcs[:, 0:1]
        exp_r = jnp.exp(rcs)
        wb = dt_b * jnp.exp(r_first - rcs)
        dec = jnp.exp(r_first)
        ys = []
        for g in range(SSD_GROUPS):
            bt = xb_ref[0, SSD_WIDTH + g * D_STATE:SSD_WIDTH + (g + 1) * D_STATE, cols]
            ct = xb_ref[0, SSD_WIDTH + (SSD_GROUPS + g) * D_STATE:
                        SSD_WIDTH + (SSD_GROUPS + g + 1) * D_STATE, cols]
            bg = bt.T.astype(BF16)
            st = sb_sc[g]
            yoff = _dot(st.astype(BF16), ct.astype(BF16))
            xws = []
            for k in range(HEADS_PER_GROUP):
                hh = g * HEADS_PER_GROUP + k
                xh = xb_ref[0, hh * SSD_HEAD_DIM:(hh + 1) * SSD_HEAD_DIM, cols]
                ys.append(yoff[k * SSD_HEAD_DIM:(k + 1) * SSD_HEAD_DIM] * exp_r[hh:hh + 1])
                xws.append(xh * wb[hh:hh + 1])
            xw = jnp.concatenate(xws, axis=0).astype(BF16)
            sb_sc[g] = st * rows64(dec, g * HEADS_PER_GROUP) + _dot(xw, bg)
        yb_ref[0, :, cols] = jnp.concatenate(ys, axis=0)


def _ssd(xt, dtt, alog_col, dskip_col):
    b, c, l = xt.shape
    ts = TS_SSD
    ns = l // ts
    return pl.pallas_call(
        functools.partial(_ssd_kernel, nck=ts // SSD_CHUNK),
        grid=(b, ns),
        in_specs=[
            pl.BlockSpec((1, c, ts), lambda bi, i: (bi, 0, i)),
            pl.BlockSpec((1, 2 * SSD_HEADS, ts), lambda bi, i: (bi, 0, i)),
            pl.BlockSpec((1, c, ts), lambda bi, i: (bi, 0, ns - 1 - i)),
            pl.BlockSpec((1, 2 * SSD_HEADS, ts), lambda bi, i: (bi, 0, ns - 1 - i)),
            pl.BlockSpec((2 * SSD_HEADS, 1), lambda bi, i: (0, 0)),
            pl.BlockSpec((SSD_WIDTH, 1), lambda bi, i: (0, 0)),
        ],
        out_specs=[
            pl.BlockSpec((1, SSD_WIDTH, ts), lambda bi, i: (bi, 0, i)),
            pl.BlockSpec((1, SSD_WIDTH, ts), lambda bi, i: (bi, 0, ns - 1 - i)),
        ],
        out_shape=[
            jax.ShapeDtypeStruct((b, SSD_WIDTH, l), F32),
            jax.ShapeDtypeStruct((b, SSD_WIDTH, l), F32),
        ],
        scratch_shapes=[pltpu.VMEM((SSD_GROUPS, GROUP_WIDTH, D_STATE), F32),
                        pltpu.VMEM((SSD_GROUPS, GROUP_WIDTH, D_STATE), F32)],
        compiler_params=pltpu.CompilerParams(
            dimension_semantics=("parallel", "arbitrary"), vmem_limit_bytes=VMEM_LIMIT),
        name="ssd",
    )(xt, dtt, xt, dtt, alog_col, dskip_col)


def _rms(x, w):
    ms = jnp.mean(x * x, axis=-1, keepdims=True)
    return x * lax.rsqrt(ms + EPS) * w


def _out_ffn_kernel(x_ref, attn_ref, yf_ref, yb_ref, z_ref, snw_ref, wo_ref, fnw_ref,
                    wg_ref, wu_ref, wd_ref, finw_ref, o_ref, *, final, d_ff):
    x = x_ref[0]
    y = (yf_ref[0] + yb_ref[0]).T
    z = z_ref[0]
    y = y * (z * _sigmoid(z))
    yn = jnp.concatenate(
        [_rms(y[:, g * GROUP_WIDTH:(g + 1) * GROUP_WIDTH], 1.0) for g in range(SSD_GROUPS)], axis=1)
    yn = (yn * snw_ref[...]).astype(BF16)
    mixed = jnp.concatenate([attn_ref[0], yn], axis=1)
    x1 = x + _dot(mixed, wo_ref[...])
    h = _rms(x1, fnw_ref[...]).astype(BF16)
    acc = x1
    for c0, c1 in ((0, FF_SPLIT), (FF_SPLIT, d_ff)):
        gt = _dot(h, wg_ref[:, c0:c1])
        up = _dot(h, wu_ref[:, c0:c1])
        act = (gt * _sigmoid(gt) * up).astype(BF16)
        acc = acc + _dot(act, wd_ref[c0:c1, :])
    if final:
        acc = _rms(acc, finw_ref[...])
    o_ref[0] = acc


def _out_ffn(x, attn, yf, yb, z, snw, wo, fnw, wg, wu, wd, finw, final):
    b, l, d = x.shape
    d_ff = wg.shape[1]
    tm = TM_FFN
    resident = lambda shape: pl.BlockSpec(shape, lambda bi, i: (0,) * len(shape),
                                          pipeline_mode=pl.Buffered(1))
    return pl.pallas_call(
        functools.partial(_out_ffn_kernel, final=final, d_ff=d_ff),
        grid=(b, l // tm),
        in_specs=[
            pl.BlockSpec((1, tm, d), lambda bi, i: (bi, i, 0)),
            pl.BlockSpec((1, tm, ATTN_WIDTH), lambda bi, i: (bi, i, 0)),
            pl.BlockSpec((1, SSD_WIDTH, tm), lambda bi, i: (bi, 0, i)),
            pl.BlockSpec((1, SSD_WIDTH, tm), lambda bi, i: (bi, 0, i)),
            pl.BlockSpec((1, tm, SSD_WIDTH), lambda bi, i: (bi, i, 0)),
            resident((1, SSD_WIDTH)),
            resident((ATTN_WIDTH + SSD_WIDTH, d)),
            resident((1, d)),
            resident((d, d_ff)),
            resident((d, d_ff)),
            resident((d_ff, d)),
            resident((1, d)),
        ],
        out_specs=pl.BlockSpec((1, tm, d), lambda bi, i: (bi, i, 0)),
        out_shape=jax.ShapeDtypeStruct((b, l, d), F32),
        compiler_params=pltpu.CompilerParams(
            dimension_semantics=("parallel", "parallel"), vmem_limit_bytes=VMEM_LIMIT),
        name="out_ffn",
    )(x, attn, yf, yb, z, snw, wo, fnw, wg, wu, wd, finw)


def _rope_tables(seq_len):
    t = jnp.arange(seq_len)
    row = (t // GRID_W).astype(F32)
    col = (t % GRID_W).astype(F32)
    half = ROPE_AXIS_DIM // 2
    inv_freq = ROPE_THETA ** (-(2.0 * jnp.arange(half, dtype=F32)) / ROPE_AXIS_DIM)
    ang_r = row[:, None] * inv_freq[None, :]
    ang_c = col[:, None] * inv_freq[None, :]
    cos = jnp.concatenate([jnp.cos(ang_r), jnp.cos(ang_r), jnp.cos(ang_c), jnp.cos(ang_c)], axis=1)
    sin = jnp.concatenate([-jnp.sin(ang_r), jnp.sin(ang_r), -jnp.sin(ang_c), jnp.sin(ang_c)], axis=1)
    reps = LANES // HEAD_DIM
    return jnp.tile(cos, (1, reps)), jnp.tile(sin, (1, reps))


def kernel(x, norm_mix_w, w_in, q_norm_w, k_norm_w, conv_w, conv_b, dt_bias, a_log, d_skip,
           ssd_norm_w, w_out, norm_ffn_w, w_gate, w_up, w_down, final_norm_w):
    b, l, d = x.shape
    depth = w_in.shape[0]
    assert l % TM_MIX == 0 and l % TS_SSD == 0 and l % TC_CONV == 0 and l % TQ_ATTN == 0
    assert l % GRID_W == 0 and w_in.shape[2] == OFF_DT + 2 * SSD_HEADS
    cos, sin = _rope_tables(l)
    finw = final_norm_w.reshape(1, d)
    for i in range(depth):
        w_in_pad = jnp.pad(w_in[i], ((0, 0), (0, D_IN_PAD - w_in.shape[2]))).astype(BF16)
        qkw = jnp.concatenate([jnp.tile(q_norm_w[i], N_Q_HEADS) * (1.0 / math.sqrt(HEAD_DIM)),
                               jnp.tile(k_norm_w[i], N_KV_HEADS)]).reshape(1, QK_WIDTH)
        dtb = jnp.pad(dt_bias[i].reshape(1, 2 * SSD_HEADS), ((0, 0), (0, LANES - 2 * SSD_HEADS)))
        q, kt, v, z, xbc_raw, dt = _mix_in(x, norm_mix_w[i].reshape(1, d), w_in_pad, qkw, cos, sin, dtb)
        attn = _attention(q, kt, v)
        xt, dtt = _conv(xbc_raw, dt, conv_w[i], conv_b[i].reshape(1, CONV_CH))
        yf, yb = _ssd(xt, dtt, a_log[i].reshape(2 * SSD_HEADS, 1),
                      jnp.repeat(d_skip[i], SSD_HEAD_DIM).reshape(SSD_WIDTH, 1))
        x = _out_ffn(x, attn, yf, yb, z, ssd_norm_w[i].reshape(1, SSD_WIDTH), w_out[i].astype(BF16),
                     norm_ffn_w[i].reshape(1, d), w_gate[i].astype(BF16), w_up[i].astype(BF16),
                     w_down[i].astype(BF16), finw, final=(i == depth - 1))
    return x
```

```python
import functools
import math

import jax
import jax.numpy as jnp
from jax import lax
from jax.experimental import pallas as pl
from jax.experimental.pallas import tpu as pltpu

F32 = jnp.float32
BF16 = jnp.bfloat16

HEAD_DIM = 64
N_Q_HEADS = 8
N_KV_HEADS = 2
ATTN_WIDTH = N_Q_HEADS * HEAD_DIM
KV_WIDTH = N_KV_HEADS * HEAD_DIM
QK_WIDTH = ATTN_WIDTH + KV_WIDTH
SSD_HEADS = 8
SSD_HEAD_DIM = 64
SSD_WIDTH = SSD_HEADS * SSD_HEAD_DIM
SSD_GROUPS = 2
HEADS_PER_GROUP = SSD_HEADS // SSD_GROUPS
GROUP_WIDTH = SSD_WIDTH // SSD_GROUPS
D_STATE = 128
D_CONV = 5
CONV_PAD = D_CONV // 2
CONV_CH = SSD_WIDTH + 2 * SSD_GROUPS * D_STATE
GRID_W = 64
ROPE_THETA = 10000.0
ROPE_AXIS_DIM = HEAD_DIM // 2
EPS = 1e-6
LOG2_E = math.log2(math.e)

LANES = 128
SUBLANES = 8
VMEM_LIMIT = 56 * 1024 * 1024

OFF_QK = 0
OFF_V = QK_WIDTH
OFF_Z = OFF_V + KV_WIDTH
OFF_XBC = OFF_Z + SSD_WIDTH
OFF_DT = OFF_XBC + CONV_CH
D_IN_PAD = OFF_DT + LANES

TM_MIX = 512
TQ_ATTN = 256
TC_CONV = 512
SSD_CHUNK = 128
TS_SSD = 512
TM_FFN = 512
FF_SPLIT = 1536


def _sigmoid(x):
    return 1.0 / (1.0 + jnp.exp(-x))


def _split3(a):
    hi = a.astype(BF16)
    r1 = a - hi.astype(F32)
    mid = r1.astype(BF16)
    lo = (r1 - mid.astype(F32)).astype(BF16)
    return hi, mid, lo


def _dot(a, b):
    return jnp.dot(a, b, preferred_element_type=F32)


def _dot_exact(a, m01):
    hi, mid, lo = _split3(a)
    return _dot(hi, m01) + _dot(mid, m01) + _dot(lo, m01)


def _mix_in_kernel(x_ref, nw_ref, w_ref, qkw_ref, cos_ref, sin_ref, dtb_ref,
                   q_ref, kt_ref, v_ref, z_ref, xbc_ref, dt_ref):
    x = x_ref[0]
    ms = jnp.mean(x * x, axis=-1, keepdims=True)
    h = (x * lax.rsqrt(ms + EPS) * nw_ref[...]).astype(BF16)
    proj = _dot(h, w_ref[...])

    qk = proj[:, OFF_QK:OFF_QK + QK_WIDTH]
    n_heads = QK_WIDTH // HEAD_DIM
    ch = lax.broadcasted_iota(jnp.int32, (QK_WIDTH, LANES), 0) // HEAD_DIM
    hd = lax.broadcasted_iota(jnp.int32, (QK_WIDTH, LANES), 1)
    ind = jnp.where(ch == hd, 1.0, 0.0).astype(BF16)
    hd_t = lax.broadcasted_iota(jnp.int32, (LANES, QK_WIDTH), 0)
    ch_t = lax.broadcasted_iota(jnp.int32, (LANES, QK_WIDTH), 1) // HEAD_DIM
    ind_t = jnp.where(ch_t == hd_t, 1.0, 0.0).astype(BF16)
    ss = _dot_exact(qk * qk, ind)
    inv = lax.rsqrt(ss * (1.0 / HEAD_DIM) + EPS)
    inv_full = _dot_exact(inv, ind_t)
    qkn = qk * inv_full * qkw_ref[...]

    cos = cos_ref[...]
    sin = sin_ref[...]
    lane = lax.broadcasted_iota(jnp.int32, cos.shape, 1)
    first_half = (lane & (ROPE_AXIS_DIM // 2)) == 0
    rot = []
    for c in range(QK_WIDTH // LANES):
        xc = qkn[:, c * LANES:(c + 1) * LANES]
        partner = jnp.where(first_half,
                            pltpu.roll(xc, LANES - ROPE_AXIS_DIM // 2, 1),
                            pltpu.roll(xc, ROPE_AXIS_DIM // 2, 1))
        rot.append(xc * cos + partner * sin)

    for c in range(ATTN_WIDTH // LANES):
        q_ref[0, 2 * c] = rot[c][:, :HEAD_DIM].astype(BF16)
        q_ref[0, 2 * c + 1] = rot[c][:, HEAD_DIM:].astype(BF16)
    kt = rot[ATTN_WIDTH // LANES].T
    kt_ref[0, 0, 0] = kt[:HEAD_DIM].astype(BF16)
    kt_ref[0, 1, 0] = kt[HEAD_DIM:].astype(BF16)

    vv = proj[:, OFF_V:OFF_V + KV_WIDTH]
    ones_col = jnp.where(lane == HEAD_DIM, 1.0, 0.0)
    v_ref[0, 0] = jnp.where(lane < HEAD_DIM, vv, ones_col).astype(BF16)
    v_ref[0, 1] = jnp.where(lane < HEAD_DIM, pltpu.roll(vv, HEAD_DIM, 1), ones_col).astype(BF16)

    z_ref[0] = proj[:, OFF_Z:OFF_Z + SSD_WIDTH]
    xbc_ref[0] = proj[:, OFF_XBC:OFF_XBC + CONV_CH]
    dtp = proj[:, OFF_DT:OFF_DT + LANES] + dtb_ref[...]
    dt_ref[0] = jnp.maximum(dtp, 0.0) + jnp.log1p(jnp.exp(-jnp.abs(dtp)))


def _mix_in(x, nw, w_in_pad, qkw, cos, sin, dtb):
    b, l, d = x.shape
    tm = TM_MIX
    nt = l // tm
    const = lambda shape: pl.BlockSpec(shape, lambda bi, i: (0,) * len(shape))
    return pl.pallas_call(
        _mix_in_kernel,
        grid=(b, nt),
        in_specs=[
            pl.BlockSpec((1, tm, d), lambda bi, i: (bi, i, 0)),
            const((1, d)),
            const((d, D_IN_PAD)),
            const((1, QK_WIDTH)),
            pl.BlockSpec((tm, LANES), lambda bi, i: (i, 0)),
            pl.BlockSpec((tm, LANES), lambda bi, i: (i, 0)),
            const((1, LANES)),
        ],
        out_specs=[
            pl.BlockSpec((1, N_Q_HEADS, tm, HEAD_DIM), lambda bi, i: (bi, 0, i, 0)),
            pl.BlockSpec((1, N_KV_HEADS, 1, HEAD_DIM, tm), lambda bi, i: (bi, 0, i, 0, 0)),
            pl.BlockSpec((1, N_KV_HEADS, tm, LANES), lambda bi, i: (bi, 0, i, 0)),
            pl.BlockSpec((1, tm, SSD_WIDTH), lambda bi, i: (bi, i, 0)),
            pl.BlockSpec((1, tm, CONV_CH), lambda bi, i: (bi, i, 0)),
            pl.BlockSpec((1, tm, LANES), lambda bi, i: (bi, i, 0)),
        ],
        out_shape=[
            jax.ShapeDtypeStruct((b, N_Q_HEADS, l, HEAD_DIM), BF16),
            jax.ShapeDtypeStruct((b, N_KV_HEADS, nt, HEAD_DIM, tm), BF16),
            jax.ShapeDtypeStruct((b, N_KV_HEADS, l, LANES), BF16),
            jax.ShapeDtypeStruct((b, l, SSD_WIDTH), F32),
            jax.ShapeDtypeStruct((b, l, CONV_CH), F32),
            jax.ShapeDtypeStruct((b, l, LANES), F32),
        ],
        compiler_params=pltpu.CompilerParams(
            dimension_semantics=("parallel", "parallel"), vmem_limit_bytes=VMEM_LIMIT),
        name="mix_in",
    )(x, nw, w_in_pad, qkw, cos, sin, dtb)


def _attn_kernel(q_ref, kt_ref, v_ref, o_ref, *, nk, tk, tq):
    q = q_ref[0].reshape(2 * tq, HEAD_DIM)
    m = jnp.full((2 * tq, LANES), -jnp.inf, F32)
    acc = jnp.zeros((2 * tq, LANES), F32)
    for c in range(nk):
        s = _dot(q, kt_ref[0, 0, c])
        m_new = jnp.maximum(m, jnp.max(s, axis=-1, keepdims=True))
        p = jnp.exp2(s - pltpu.repeat(m_new, tk // LANES, 1)).astype(BF16)
        acc = jnp.exp2(m - m_new) * acc + _dot(p, v_ref[0, 0, c * tk:(c + 1) * tk, :])
        m = m_new
    o = acc * (1.0 / acc[:, HEAD_DIM:HEAD_DIM + 1])
    lane = lax.broadcasted_iota(jnp.int32, (tq, LANES), 1)
    o_ref[0] = jnp.where(lane < HEAD_DIM, o[:tq], pltpu.roll(o[tq:], HEAD_DIM, 1)).astype(BF16)


def _attention(q, kt, v):
    b, _, l, _ = q.shape
    nk, tk = kt.shape[2], kt.shape[4]
    tq = TQ_ATTN
    pairs_per_kv = N_Q_HEADS // N_KV_HEADS // 2
    return pl.pallas_call(
        functools.partial(_attn_kernel, nk=nk, tk=tk, tq=tq),
        grid=(b, N_Q_HEADS // 2, l // tq),
        in_specs=[
            pl.BlockSpec((1, 2, tq, HEAD_DIM), lambda bi, hp, i: (bi, hp, i, 0)),
            pl.BlockSpec((1, 1, nk, HEAD_DIM, tk), lambda bi, hp, i: (bi, hp // pairs_per_kv, 0, 0, 0)),
            pl.BlockSpec((1, 1, l, LANES), lambda bi, hp, i: (bi, hp // pairs_per_kv, 0, 0)),
        ],
        out_specs=pl.BlockSpec((1, tq, LANES), lambda bi, hp, i: (bi, i, hp)),
        out_shape=jax.ShapeDtypeStruct((b, l, ATTN_WIDTH), BF16),
        compiler_params=pltpu.CompilerParams(
            dimension_semantics=("parallel", "parallel", "parallel"), vmem_limit_bytes=VMEM_LIMIT),
        name="attn",
    )(q, kt, v)


def _conv_kernel(prev_ref, cur_ref, next_ref, dt_ref, w_ref, b_ref, xt_ref, dtt_ref, ext_sc, *, tc):
    i = pl.program_id(1)
    has_prev = (i > 0).astype(F32)
    has_next = (i < pl.num_programs(1) - 1).astype(F32)
    ext_sc[0:SUBLANES] = prev_ref[0] * has_prev
    ext_sc[SUBLANES:SUBLANES + tc] = cur_ref[0]
    ext_sc[SUBLANES + tc:2 * SUBLANES + tc] = next_ref[0] * has_next
    w = w_ref[...]
    acc = jnp.broadcast_to(b_ref[...], (tc, CONV_CH))
    for k in range(D_CONV):
        off = SUBLANES - CONV_PAD + k
        acc = acc + ext_sc[off:off + tc] * w[k:k + 1]
    y = acc * _sigmoid(acc)
    xt_ref[0] = y.T
    dtt_ref[0] = dt_ref[0].T[:2 * SSD_HEADS]


def _conv(xbc_raw, dt, conv_w, conv_b):
    b, l, c = xbc_raw.shape
    tc = TC_CONV
    rb = tc // SUBLANES
    last = l // SUBLANES - 1
    return pl.pallas_call(
        functools.partial(_conv_kernel, tc=tc),
        grid=(b, l // tc),
        in_specs=[
            pl.BlockSpec((1, SUBLANES, c), lambda bi, i: (bi, jnp.maximum(i * rb - 1, 0), 0)),
            pl.BlockSpec((1, tc, c), lambda bi, i: (bi, i, 0)),
            pl.BlockSpec((1, SUBLANES, c), lambda bi, i: (bi, jnp.minimum((i + 1) * rb, last), 0)),
            pl.BlockSpec((1, tc, LANES), lambda bi, i: (bi, i, 0)),
            pl.BlockSpec((D_CONV, c), lambda bi, i: (0, 0)),
            pl.BlockSpec((1, c), lambda bi, i: (0, 0)),
        ],
        out_specs=[
            pl.BlockSpec((1, c, tc), lambda bi, i: (bi, 0, i)),
            pl.BlockSpec((1, 2 * SSD_HEADS, tc), lambda bi, i: (bi, 0, i)),
        ],
        out_shape=[
            jax.ShapeDtypeStruct((b, c, l), F32),
            jax.ShapeDtypeStruct((b, 2 * SSD_HEADS, l), F32),
        ],
        scratch_shapes=[pltpu.VMEM((tc + 2 * SUBLANES, c), F32)],
        compiler_params=pltpu.CompilerParams(
            dimension_semantics=("parallel", "parallel"), vmem_limit_bytes=VMEM_LIMIT),
        name="conv",
    )(xbc_raw, xbc_raw, xbc_raw, dt, conv_w, conv_b)


def _ssd_kernel(xf_ref, dtf_ref, xb_ref, dtb_ref, alog_ref, dskip_ref, yf_ref, yb_ref,
                sf_sc, sb_sc, *, nck):
    q = SSD_CHUNK

    @pl.when(pl.program_id(1) == 0)
    def _():
        sf_sc[...] = jnp.zeros(sf_sc.shape, F32)
        sb_sc[...] = jnp.zeros(sb_sc.shape, F32)

    a16 = -jnp.exp(alog_ref[...])
    r = lax.broadcasted_iota(jnp.int32, (q, q), 0)
    c = lax.broadcasted_iota(jnp.int32, (q, q), 1)
    fwd_mask = c >= r
    bwd_mask = c <= r
    incl_fwd = jnp.where(fwd_mask, 1.0, 0.0).astype(BF16)
    incl_rev = jnp.where(bwd_mask, 1.0, 0.0).astype(BF16)
    row16 = lax.broadcasted_iota(jnp.int32, (2 * SSD_HEADS, q), 0)
    dsk = dskip_ref[...]

    def rows64(v8, h0):
        return jnp.concatenate(
            [jnp.broadcast_to(v8[h0 + k:h0 + k + 1], (SSD_HEAD_DIM, 1)) for k in range(HEADS_PER_GROUP)],
            axis=0)

    for j in range(nck):
        cols = slice(j * q, (j + 1) * q)
        dt_t = dtf_ref[0, :, cols]
        ac = dt_t * a16
        cs = (_dot_exact(jnp.where(row16 < SSD_HEADS, ac, 0.0), incl_fwd)
              + _dot_exact(jnp.where(row16 >= SSD_HEADS, ac, 0.0), incl_rev))
        cs_col = jnp.concatenate([cs, jnp.zeros((q - 2 * SSD_HEADS, q), F32)], axis=0).T
        a_last = cs[:SSD_HEADS, q - 1:q]
        exp_a = jnp.exp(cs[:SSD_HEADS])
        wf = dt_t[:SSD_HEADS] * jnp.exp(a_last - cs[:SSD_HEADS])
        dec = jnp.exp(a_last)
        ys = []
        for g in range(SSD_GROUPS):
            bt = xf_ref[0, SSD_WIDTH + g * D_STATE:SSD_WIDTH + (g + 1) * D_STATE, cols]
            ct = xf_ref[0, SSD_WIDTH + (SSD_GROUPS + g) * D_STATE:
                        SSD_WIDTH + (SSD_GROUPS + g + 1) * D_STATE, cols]
            bg = bt.T.astype(BF16)
            ctb = ct.astype(BF16)
            sc_t = _dot(bg, ctb)
            st = sf_sc[g]
            yoff = _dot(st.astype(BF16), ctb)
            xws = []
            for k in range(HEADS_PER_GROUP):
                hh = g * HEADS_PER_GROUP + k
                xh = xf_ref[0, hh * SSD_HEAD_DIM:(hh + 1) * SSD_HEAD_DIM, cols]
                arg = jnp.where(fwd_mask,
                                cs[hh:hh + 1] - cs_col[:, hh:hh + 1],
                                cs[SSD_HEADS + hh:SSD_HEADS + hh + 1]
                                - cs_col[:, SSD_HEADS + hh:SSD_HEADS + hh + 1])
                e = jnp.exp(arg)
                up = (jnp.where(fwd_mask, e, 0.0) * sc_t).astype(BF16)
                lo = (jnp.where(bwd_mask, e, 0.0) * sc_t).astype(BF16)
                lhs = jnp.concatenate([xh * dt_t[hh:hh + 1], xh * dt_t[SSD_HEADS + hh:SSD_HEADS + hh + 1]],
                                      axis=1).astype(BF16)
                yd = _dot(lhs, jnp.concatenate([up, lo], axis=0))
                ys.append(yd + yoff[k * SSD_HEAD_DIM:(k + 1) * SSD_HEAD_DIM] * exp_a[hh:hh + 1]
                          + xh * dsk[hh * SSD_HEAD_DIM:(hh + 1) * SSD_HEAD_DIM])
                xws.append(xh * wf[hh:hh + 1])
            xw = jnp.concatenate(xws, axis=0).astype(BF16)
            sf_sc[g] = st * rows64(dec, g * HEADS_PER_GROUP) + _dot(xw, bg)
        yf_ref[0, :, cols] = jnp.concatenate(ys, axis=0)

    for j in reversed(range(nck)):
        cols = slice(j * q, (j + 1) * q)
        dt_b = dtb_ref[0, SSD_HEADS:, cols]
        rcs = _dot_exact(dt_b * a16[SSD_HEADS:], incl_rev)
        r_first = rcs[:, 0:1]
        exp_r = jnp.exp(rcs)
        wb = dt_b * jnp.exp(r_first - rcs)
        dec = jnp.exp(r_first)
        ys = []
        for g in range(SSD_GROUPS):
            bt = xb_ref[0, SSD_WIDTH + g * D_STATE:SSD_WIDTH + (g + 1) * D_STATE, cols]
            ct = xb_ref[0, SSD_WIDTH + (SSD_GROUPS + g) * D_STATE:
                        SSD_WIDTH + (SSD_GROUPS + g + 1) * D_STATE, cols]
            bg = bt.T.astype(BF16)
            st = sb_sc[g]
            yoff = _dot(st.astype(BF16), ct.astype(BF16))
            xws = []
            for k in range(HEADS_PER_GROUP):
                hh = g * HEADS_PER_GROUP + k
                xh = xb_ref[0, hh * SSD_HEAD_DIM:(hh + 1) * SSD_HEAD_DIM, cols]
                ys.append(yoff[k * SSD_HEAD_DIM:(k + 1) * SSD_HEAD_DIM] * exp_r[hh:hh + 1])
                xws.append(xh * wb[hh:hh + 1])
            xw = jnp.concatenate(xws, axis=0).astype(BF16)
            sb_sc[g] = st * rows64(dec, g * HEADS_PER_GROUP) + _dot(xw, bg)
        yb_ref[0, :, cols] = jnp.concatenate(ys, axis=0)


def _ssd(xt, dtt, alog_col, dskip_col):
    b, c, l = xt.shape
    ts = TS_SSD
    ns = l // ts
    return pl.pallas_call(
        functools.partial(_ssd_kernel, nck=ts // SSD_CHUNK),
        grid=(b, ns),
        in_specs=[
            pl.BlockSpec((1, c, ts), lambda bi, i: (bi, 0, i)),
            pl.BlockSpec((1, 2 * SSD_HEADS, ts), lambda bi, i: (bi, 0, i)),
            pl.BlockSpec((1, c, ts), lambda bi, i: (bi, 0, ns - 1 - i)),
            pl.BlockSpec((1, 2 * SSD_HEADS, ts), lambda bi, i: (bi, 0, ns - 1 - i)),
            pl.BlockSpec((2 * SSD_HEADS, 1), lambda bi, i: (0, 0)),
            pl.BlockSpec((SSD_WIDTH, 1), lambda bi, i: (0, 0)),
        ],
        out_specs=[
            pl.BlockSpec((1, SSD_WIDTH, ts), lambda bi, i: (bi, 0, i)),
            pl.BlockSpec((1, SSD_WIDTH, ts), lambda bi, i: (bi, 0, ns - 1 - i)),
        ],
        out_shape=[
            jax.ShapeDtypeStruct((b, SSD_WIDTH, l), F32),
            jax.ShapeDtypeStruct((b, SSD_WIDTH, l), F32),
        ],
        scratch_shapes=[pltpu.VMEM((SSD_GROUPS, GROUP_WIDTH, D_STATE), F32),
                        pltpu.VMEM((SSD_GROUPS, GROUP_WIDTH, D_STATE), F32)],
        compiler_params=pltpu.CompilerParams(
            dimension_semantics=("parallel", "arbitrary"), vmem_limit_bytes=VMEM_LIMIT),
        name="ssd",
    )(xt, dtt, xt, dtt, alog_col, dskip_col)


def _rms(x, w):
    ms = jnp.mean(x * x, axis=-1, keepdims=True)
    return x * lax.rsqrt(ms + EPS) * w


def _out_ffn_kernel(x_ref, attn_ref, yf_ref, yb_ref, z_ref, snw_ref, wo_ref, fnw_ref,
                    wg_ref, wu_ref, wd_ref, finw_ref, o_ref, *, final, d_ff):
    x = x_ref[0]
    y = (yf_ref[0] + yb_ref[0]).T
    z = z_ref[0]
    y = y * (z * _sigmoid(z))
    yn = jnp.concatenate(
        [_rms(y[:, g * GROUP_WIDTH:(g + 1) * GROUP_WIDTH], 1.0) for g in range(SSD_GROUPS)], axis=1)
    yn = (yn * snw_ref[...]).astype(BF16)
    mixed = jnp.concatenate([attn_ref[0], yn], axis=1)
    x1 = x + _dot(mixed, wo_ref[...])
    h = _rms(x1, fnw_ref[...]).astype(BF16)
    acc = x1
    for c0, c1 in ((0, FF_SPLIT), (FF_SPLIT, d_ff)):
        gt = _dot(h, wg_ref[:, c0:c1])
        up = _dot(h, wu_ref[:, c0:c1])
        act = (gt * _sigmoid(gt) * up).astype(BF16)
        acc = acc + _dot(act, wd_ref[c0:c1, :])
    if final:
        acc = _rms(acc, finw_ref[...])
    o_ref[0] = acc


def _out_ffn(x, attn, yf, yb, z, snw, wo, fnw, wg, wu, wd, finw, final):
    b, l, d = x.shape
    d_ff = wg.shape[1]
    tm = TM_FFN
    resident = lambda shape: pl.BlockSpec(shape, lambda bi, i: (0,) * len(shape),
                                          pipeline_mode=pl.Buffered(1))
    return pl.pallas_call(
        functools.partial(_out_ffn_kernel, final=final, d_ff=d_ff),
        grid=(b, l // tm),
        in_specs=[
            pl.BlockSpec((1, tm, d), lambda bi, i: (bi, i, 0)),
            pl.BlockSpec((1, tm, ATTN_WIDTH), lambda bi, i: (bi, i, 0)),
            pl.BlockSpec((1, SSD_WIDTH, tm), lambda bi, i: (bi, 0, i)),
            pl.BlockSpec((1, SSD_WIDTH, tm), lambda bi, i: (bi, 0, i)),
            pl.BlockSpec((1, tm, SSD_WIDTH), lambda bi, i: (bi, i, 0)),
            resident((1, SSD_WIDTH)),
            resident((ATTN_WIDTH + SSD_WIDTH, d)),
            resident((1, d)),
            resident((d, d_ff)),
            resident((d, d_ff)),
            resident((d_ff, d)),
            resident((1, d)),
        ],
        out_specs=pl.BlockSpec((1, tm, d), lambda bi, i: (bi, i, 0)),
        out_shape=jax.ShapeDtypeStruct((b, l, d), F32),
        compiler_params=pltpu.CompilerParams(
            dimension_semantics=("parallel", "parallel"), vmem_limit_bytes=VMEM_LIMIT),
        name="out_ffn",
    )(x, attn, yf, yb, z, snw, wo, fnw, wg, wu, wd, finw)


def _rope_tables(seq_len):
    t = jnp.arange(seq_len)
    row = (t // GRID_W).astype(F32)
    col = (t % GRID_W).astype(F32)
    half = ROPE_AXIS_DIM // 2
    inv_freq = ROPE_THETA ** (-(2.0 * jnp.arange(half, dtype=F32)) / ROPE_AXIS_DIM)
    ang_r = row[:, None] * inv_freq[None, :]
    ang_c = col[:, None] * inv_freq[None, :]
    cos = jnp.concatenate([jnp.cos(ang_r), jnp.cos(ang_r), jnp.cos(ang_c), jnp.cos(ang_c)], axis=1)
    sin = jnp.concatenate([-jnp.sin(ang_r), jnp.sin(ang_r), -jnp.sin(ang_c), jnp.sin(ang_c)], axis=1)
    reps = LANES // HEAD_DIM
    return jnp.tile(cos, (1, reps)), jnp.tile(sin, (1, reps))


def kernel(x, norm_mix_w, w_in, q_norm_w, k_norm_w, conv_w, conv_b, dt_bias, a_log, d_skip,
           ssd_norm_w, w_out, norm_ffn_w, w_gate, w_up, w_down, final_norm_w):
    b, l, d = x.shape
    depth = w_in.shape[0]
    assert l % TM_MIX == 0 and l % TS_SSD == 0 and l % TC_CONV == 0 and l % TQ_ATTN == 0
    assert l % GRID_W == 0 and w_in.shape[2] == OFF_DT + 2 * SSD_HEADS
    cos, sin = _rope_tables(l)
    finw = final_norm_w.reshape(1, d)
    for i in range(depth):
        w_in_pad = jnp.pad(w_in[i], ((0, 0), (0, D_IN_PAD - w_in.shape[2]))).astype(BF16)
        qkw = jnp.concatenate([jnp.tile(q_norm_w[i], N_Q_HEADS) * (LOG2_E / math.sqrt(HEAD_DIM)),
                               jnp.tile(k_norm_w[i], N_KV_HEADS)]).reshape(1, QK_WIDTH)
        dtb = jnp.pad(dt_bias[i].reshape(1, 2 * SSD_HEADS), ((0, 0), (0, LANES - 2 * SSD_HEADS)))
        q, kt, v, z, xbc_raw, dt = _mix_in(x, norm_mix_w[i].reshape(1, d), w_in_pad, qkw, cos, sin, dtb)
        attn = _attention(q, kt, v)
        xt, dtt = _conv(xbc_raw, dt, conv_w[i], conv_b[i].reshape(1, CONV_CH))
        yf, yb = _ssd(xt, dtt, a_log[i].reshape(2 * SSD_HEADS, 1),
                      jnp.repeat(d_skip[i], SSD_HEAD_DIM).reshape(SSD_WIDTH, 1))
        x = _out_ffn(x, attn, yf, yb, z, ssd_norm_w[i].reshape(1, SSD_WIDTH), w_out[i].astype(BF16),
                     norm_ffn_w[i].reshape(1, d), w_gate[i].astype(BF16), w_up[i].astype(BF16),
                     w_down[i].astype(BF16), finw, final=(i == depth - 1))
    return x
```

```python
import functools
import math

import jax
import jax.numpy as jnp
from jax import lax
from jax.experimental import pallas as pl
from jax.experimental.pallas import tpu as pltpu

F32 = jnp.float32
BF16 = jnp.bfloat16

HEAD_DIM = 64
N_Q_HEADS = 8
N_KV_HEADS = 2
ATTN_WIDTH = N_Q_HEADS * HEAD_DIM
KV_WIDTH = N_KV_HEADS * HEAD_DIM
QK_WIDTH = ATTN_WIDTH + KV_WIDTH
SSD_HEADS = 8
SSD_HEAD_DIM = 64
SSD_WIDTH = SSD_HEADS * SSD_HEAD_DIM
SSD_GROUPS = 2
HEADS_PER_GROUP = SSD_HEADS // SSD_GROUPS
GROUP_WIDTH = SSD_WIDTH // SSD_GROUPS
D_STATE = 128
D_CONV = 5
CONV_PAD = D_CONV // 2
CONV_CH = SSD_WIDTH + 2 * SSD_GROUPS * D_STATE
GRID_W = 64
ROPE_THETA = 10000.0
ROPE_AXIS_DIM = HEAD_DIM // 2
EPS = 1e-6
LOG2_E = math.log2(math.e)

LANES = 128
SUBLANES = 8
VMEM_LIMIT = 56 * 1024 * 1024

OFF_QK = 0
OFF_V = QK_WIDTH
OFF_Z = OFF_V + KV_WIDTH
OFF_XBC = OFF_Z + SSD_WIDTH
OFF_DT = OFF_XBC + CONV_CH
D_IN_PAD = OFF_DT + LANES

TM_MIX = 512
TQ_ATTN = 512
TC_CONV = 512
SSD_CHUNK = 128
TS_SSD = 512
TM_FFN = 512
FF_SPLIT = 1536


def _sigmoid(x):
    return 1.0 / (1.0 + jnp.exp(-x))


def _split3(a):
    hi = a.astype(BF16)
    r1 = a - hi.astype(F32)
    mid = r1.astype(BF16)
    lo = (r1 - mid.astype(F32)).astype(BF16)
    return hi, mid, lo


def _dot(a, b):
    return jnp.dot(a, b, preferred_element_type=F32)


def _dot_exact(a, m01):
    hi, mid, lo = _split3(a)
    return _dot(hi, m01) + _dot(mid, m01) + _dot(lo, m01)


def _mix_in_kernel(x_ref, nw_ref, w_ref, qkw_ref, cos_ref, sin_ref, dtb_ref,
                   q_ref, kt_ref, v_ref, z_ref, xbc_ref, dt_ref):
    x = x_ref[0]
    ms = jnp.mean(x * x, axis=-1, keepdims=True)
    h = (x * lax.rsqrt(ms + EPS) * nw_ref[...]).astype(BF16)
    proj = _dot(h, w_ref[...])

    qk = proj[:, OFF_QK:OFF_QK + QK_WIDTH]
    n_heads = QK_WIDTH // HEAD_DIM
    ch = lax.broadcasted_iota(jnp.int32, (QK_WIDTH, LANES), 0) // HEAD_DIM
    hd = lax.broadcasted_iota(jnp.int32, (QK_WIDTH, LANES), 1)
    ind = jnp.where(ch == hd, 1.0, 0.0).astype(BF16)
    hd_t = lax.broadcasted_iota(jnp.int32, (LANES, QK_WIDTH), 0)
    ch_t = lax.broadcasted_iota(jnp.int32, (LANES, QK_WIDTH), 1) // HEAD_DIM
    ind_t = jnp.where(ch_t == hd_t, 1.0, 0.0).astype(BF16)
    ss = _dot_exact(qk * qk, ind)
    inv = lax.rsqrt(ss * (1.0 / HEAD_DIM) + EPS)
    inv_full = _dot_exact(inv, ind_t)
    qkn = qk * inv_full * qkw_ref[...]

    cos = cos_ref[...]
    sin = sin_ref[...]
    lane = lax.broadcasted_iota(jnp.int32, cos.shape, 1)
    first_half = (lane & (ROPE_AXIS_DIM // 2)) == 0
    rot = []
    for c in range(QK_WIDTH // LANES):
        xc = qkn[:, c * LANES:(c + 1) * LANES]
        partner = jnp.where(first_half,
                            pltpu.roll(xc, LANES - ROPE_AXIS_DIM // 2, 1),
                            pltpu.roll(xc, ROPE_AXIS_DIM // 2, 1))
        rot.append(xc * cos + partner * sin)

    for c in range(ATTN_WIDTH // LANES):
        q_ref[0, 2 * c] = rot[c][:, :HEAD_DIM].astype(BF16)
        q_ref[0, 2 * c + 1] = rot[c][:, HEAD_DIM:].astype(BF16)
    kt = rot[ATTN_WIDTH // LANES].T
    kt_ref[0, 0, 0] = kt[:HEAD_DIM].astype(BF16)
    kt_ref[0, 1, 0] = kt[HEAD_DIM:].astype(BF16)

    vv = proj[:, OFF_V:OFF_V + KV_WIDTH]
    ones_col = jnp.where(lane == HEAD_DIM, 1.0, 0.0)
    v_ref[0, 0] = jnp.where(lane < HEAD_DIM, vv, ones_col).astype(BF16)
    v_ref[0, 1] = jnp.where(lane < HEAD_DIM, pltpu.roll(vv, HEAD_DIM, 1), ones_col).astype(BF16)

    z_ref[0] = proj[:, OFF_Z:OFF_Z + SSD_WIDTH]
    xbc_ref[0] = proj[:, OFF_XBC:OFF_XBC + CONV_CH]
    dtp = proj[:, OFF_DT:OFF_DT + LANES] + dtb_ref[...]
    dt_ref[0] = jnp.maximum(dtp, 0.0) + jnp.log1p(jnp.exp(-jnp.abs(dtp)))


def _mix_in(x, nw, w_in_pad, qkw, cos, sin, dtb):
    b, l, d = x.shape
    tm = TM_MIX
    nt = l // tm
    const = lambda shape: pl.BlockSpec(shape, lambda bi, i: (0,) * len(shape))
    return pl.pallas_call(
        _mix_in_kernel,
        grid=(b, nt),
        in_specs=[
            pl.BlockSpec((1, tm, d), lambda bi, i: (bi, i, 0)),
            const((1, d)),
            const((d, D_IN_PAD)),
            const((1, QK_WIDTH)),
            pl.BlockSpec((tm, LANES), lambda bi, i: (i, 0)),
            pl.BlockSpec((tm, LANES), lambda bi, i: (i, 0)),
            const((1, LANES)),
        ],
        out_specs=[
            pl.BlockSpec((1, N_Q_HEADS, tm, HEAD_DIM), lambda bi, i: (bi, 0, i, 0)),
            pl.BlockSpec((1, N_KV_HEADS, 1, HEAD_DIM, tm), lambda bi, i: (bi, 0, i, 0, 0)),
            pl.BlockSpec((1, N_KV_HEADS, tm, LANES), lambda bi, i: (bi, 0, i, 0)),
            pl.BlockSpec((1, tm, SSD_WIDTH), lambda bi, i: (bi, i, 0)),
            pl.BlockSpec((1, tm, CONV_CH), lambda bi, i: (bi, i, 0)),
            pl.BlockSpec((1, tm, LANES), lambda bi, i: (bi, i, 0)),
        ],
        out_shape=[
            jax.ShapeDtypeStruct((b, N_Q_HEADS, l, HEAD_DIM), BF16),
            jax.ShapeDtypeStruct((b, N_KV_HEADS, nt, HEAD_DIM, tm), BF16),
            jax.ShapeDtypeStruct((b, N_KV_HEADS, l, LANES), BF16),
            jax.ShapeDtypeStruct((b, l, SSD_WIDTH), F32),
            jax.ShapeDtypeStruct((b, l, CONV_CH), F32),
            jax.ShapeDtypeStruct((b, l, LANES), F32),
        ],
        compiler_params=pltpu.CompilerParams(
            dimension_semantics=("parallel", "parallel"), vmem_limit_bytes=VMEM_LIMIT),
        name="mix_in",
    )(x, nw, w_in_pad, qkw, cos, sin, dtb)


def _attn_kernel(q_ref, kt_ref, v_ref, o_ref, *, nk, tk, tq):
    q = q_ref[0].reshape(2 * tq, HEAD_DIM)
    m = jnp.full((2 * tq, LANES), -jnp.inf, F32)
    acc = jnp.zeros((2 * tq, LANES), F32)
    for c in range(nk):
        s = _dot(q, kt_ref[0, 0, c])
        m_new = jnp.maximum(m, jnp.max(s, axis=-1, keepdims=True))
        p = jnp.exp2((s - jnp.concatenate([m_new] * (tk // LANES), axis=1)).astype(BF16))
        acc = jnp.exp2(m - m_new) * acc + _dot(p, v_ref[0, 0, c * tk:(c + 1) * tk, :])
        m = m_new
    o = acc * (1.0 / acc[:, HEAD_DIM:HEAD_DIM + 1])
    lane = lax.broadcasted_iota(jnp.int32, (tq, LANES), 1)
    o_ref[0] = jnp.where(lane < HEAD_DIM, o[:tq], pltpu.roll(o[tq:], HEAD_DIM, 1)).astype(BF16)


def _attention(q, kt, v):
    b, _, l, _ = q.shape
    nk, tk = kt.shape[2], kt.shape[4]
    tq = TQ_ATTN
    pairs_per_kv = N_Q_HEADS // N_KV_HEADS // 2
    return pl.pallas_call(
        functools.partial(_attn_kernel, nk=nk, tk=tk, tq=tq),
        grid=(b, N_Q_HEADS // 2, l // tq),
        in_specs=[
            pl.BlockSpec((1, 2, tq, HEAD_DIM), lambda bi, hp, i: (bi, hp, i, 0)),
            pl.BlockSpec((1, 1, nk, HEAD_DIM, tk), lambda bi, hp, i: (bi, hp // pairs_per_kv, 0, 0, 0)),
            pl.BlockSpec((1, 1, l, LANES), lambda bi, hp, i: (bi, hp // pairs_per_kv, 0, 0)),
        ],
        out_specs=pl.BlockSpec((1, tq, LANES), lambda bi, hp, i: (bi, i, hp)),
        out_shape=jax.ShapeDtypeStruct((b, l, ATTN_WIDTH), BF16),
        compiler_params=pltpu.CompilerParams(
            dimension_semantics=("parallel", "parallel", "parallel"), vmem_limit_bytes=VMEM_LIMIT),
        name="attn",
    )(q, kt, v)


def _conv_kernel(prev_ref, cur_ref, next_ref, dt_ref, w_ref, b_ref, xt_ref, dtt_ref, ext_sc, *, tc):
    i = pl.program_id(1)
    has_prev = (i > 0).astype(F32)
    has_next = (i < pl.num_programs(1) - 1).astype(F32)
    ext_sc[0:SUBLANES] = prev_ref[0] * has_prev
    ext_sc[SUBLANES:SUBLANES + tc] = cur_ref[0]
    ext_sc[SUBLANES + tc:2 * SUBLANES + tc] = next_ref[0] * has_next
    w = w_ref[...]
    acc = jnp.broadcast_to(b_ref[...], (tc, CONV_CH))
    for k in range(D_CONV):
        off = SUBLANES - CONV_PAD + k
        acc = acc + ext_sc[off:off + tc] * w[k:k + 1]
    y = acc * _sigmoid(acc)
    xt_ref[0] = y.T
    dtt_ref[0] = dt_ref[0].T[:2 * SSD_HEADS]


def _conv(xbc_raw, dt, conv_w, conv_b):
    b, l, c = xbc_raw.shape
    tc = TC_CONV
    rb = tc // SUBLANES
    last = l // SUBLANES - 1
    return pl.pallas_call(
        functools.partial(_conv_kernel, tc=tc),
        grid=(b, l // tc),
        in_specs=[
            pl.BlockSpec((1, SUBLANES, c), lambda bi, i: (bi, jnp.maximum(i * rb - 1, 0), 0)),
            pl.BlockSpec((1, tc, c), lambda bi, i: (bi, i, 0)),
            pl.BlockSpec((1, SUBLANES, c), lambda bi, i: (bi, jnp.minimum((i + 1) * rb, last), 0)),
            pl.BlockSpec((1, tc, LANES), lambda bi, i: (bi, i, 0)),
            pl.BlockSpec((D_CONV, c), lambda bi, i: (0, 0)),
            pl.BlockSpec((1, c), lambda bi, i: (0, 0)),
        ],
        out_specs=[
            pl.BlockSpec((1, c, tc), lambda bi, i: (bi, 0, i)),
            pl.BlockSpec((1, 2 * SSD_HEADS, tc), lambda bi, i: (bi, 0, i)),
        ],
        out_shape=[
            jax.ShapeDtypeStruct((b, c, l), F32),
            jax.ShapeDtypeStruct((b, 2 * SSD_HEADS, l), F32),
        ],
        scratch_shapes=[pltpu.VMEM((tc + 2 * SUBLANES, c), F32)],
        compiler_params=pltpu.CompilerParams(
            dimension_semantics=("parallel", "parallel"), vmem_limit_bytes=VMEM_LIMIT),
        name="conv",
    )(xbc_raw, xbc_raw, xbc_raw, dt, conv_w, conv_b)


def _ssd_kernel(xf_ref, dtf_ref, xb_ref, dtb_ref, alog_ref, dskip_ref, yf_ref, yb_ref,
                sf_sc, sb_sc, *, nck):
    q = SSD_CHUNK

    @pl.when(pl.program_id(1) == 0)
    def _():
        sf_sc[...] = jnp.zeros(sf_sc.shape, F32)
        sb_sc[...] = jnp.zeros(sb_sc.shape, F32)

    a16 = -jnp.exp(alog_ref[...])
    r = lax.broadcasted_iota(jnp.int32, (q, q), 0)
    c = lax.broadcasted_iota(jnp.int32, (q, q), 1)
    fwd_mask = c >= r
    bwd_mask = c <= r
    incl_fwd = jnp.where(fwd_mask, 1.0, 0.0).astype(BF16)
    incl_rev = jnp.where(bwd_mask, 1.0, 0.0).astype(BF16)
    row16 = lax.broadcasted_iota(jnp.int32, (2 * SSD_HEADS, q), 0)
    dsk = dskip_ref[...]

    def rows64(v8, h0):
        return jnp.concatenate(
            [jnp.broadcast_to(v8[h0 + k:h0 + k + 1], (SSD_HEAD_DIM, 1)) for k in range(HEADS_PER_GROUP)],
            axis=0)

    for j in range(nck):
        cols = slice(j * q, (j + 1) * q)
        dt_t = dtf_ref[0, :, cols]
        ac = dt_t * a16
        cs = (_dot_exact(jnp.where(row16 < SSD_HEADS, ac, 0.0), incl_fwd)
              + _dot_exact(jnp.where(row16 >= SSD_HEADS, ac, 0.0), incl_rev))
        cs_col = jnp.concatenate([cs, jnp.zeros((q - 2 * SSD_HEADS, q), F32)], axis=0).T
        a_last = cs[:SSD_HEADS, q - 1:q]
        exp_a = jnp.exp(cs[:SSD_HEADS])
        wf = dt_t[:SSD_HEADS] * jnp.exp(a_last - cs[:SSD_HEADS])
        dec = jnp.exp(a_last)
        ys = []
        for g in range(SSD_GROUPS):
            bt = xf_ref[0, SSD_WIDTH + g * D_STATE:SSD_WIDTH + (g + 1) * D_STATE, cols]
            ct = xf_ref[0, SSD_WIDTH + (SSD_GROUPS + g) * D_STATE:
                        SSD_WIDTH + (SSD_GROUPS + g + 1) * D_STATE, cols]
            bg = bt.T.astype(BF16)
            ctb = ct.astype(BF16)
            sc_t = _dot(bg, ctb)
            st = sf_sc[g]
            yoff = _dot(st.astype(BF16), ctb)
            xws = []
            for k in range(HEADS_PER_GROUP):
                hh = g * HEADS_PER_GROUP + k
                xh = xf_ref[0, hh * SSD_HEAD_DIM:(hh + 1) * SSD_HEAD_DIM, cols]
                arg = jnp.where(fwd_mask,
                                cs[hh:hh + 1] - cs_col[:, hh:hh + 1],
                                cs[SSD_HEADS + hh:SSD_HEADS + hh + 1]
                                - cs_col[:, SSD_HEADS + hh:SSD_HEADS + hh + 1])
                e = jnp.exp(arg)
                up = (jnp.where(fwd_mask, e, 0.0) * sc_t).astype(BF16)
                lo = (jnp.where(bwd_mask, e, 0.0) * sc_t).astype(BF16)
                lhs = jnp.concatenate([xh * dt_t[hh:hh + 1], xh * dt_t[SSD_HEADS + hh:SSD_HEADS + hh + 1]],
                                      axis=1).astype(BF16)
                yd = _dot(lhs, jnp.concatenate([up, lo], axis=0))
                ys.append(yd + yoff[k * SSD_HEAD_DIM:(k + 1) * SSD_HEAD_DIM] * exp_a[hh:hh + 1]
                          + xh * dsk[hh * SSD_HEAD_DIM:(hh + 1) * SSD_HEAD_DIM])
                xws.append(xh * wf[hh:hh + 1])
            xw = jnp.concatenate(xws, axis=0).astype(BF16)
            sf_sc[g] = st * rows64(dec, g * HEADS_PER_GROUP) + _dot(xw, bg)
        yf_ref[0, :, cols] = jnp.concatenate(ys, axis=0)

    for j in reversed(range(nck)):
        cols = slice(j * q, (j + 1) * q)
        dt_b = dtb_ref[0, SSD_HEADS:, cols]
        rcs = _dot_exact(dt_b * a16[SSD_HEADS:], incl_rev)
        r_first = rcs[:, 0:1]
        exp_r = jnp.exp(rcs)
        wb = dt_b * jnp.exp(r_first - rcs)
        dec = jnp.exp(r_first)
        ys = []
        for g in range(SSD_GROUPS):
            bt = xb_ref[0, SSD_WIDTH + g * D_STATE:SSD_WIDTH + (g + 1) * D_STATE, cols]
            ct = xb_ref[0, SSD_WIDTH + (SSD_GROUPS + g) * D_STATE:
                        SSD_WIDTH + (SSD_GROUPS + g + 1) * D_STATE, cols]
            bg = bt.T.astype(BF16)
            st = sb_sc[g]
            yoff = _dot(st.astype(BF16), ct.astype(BF16))
            xws = []
            for k in range(HEADS_PER_GROUP):
                hh = g * HEADS_PER_GROUP + k
                xh = xb_ref[0, hh * SSD_HEAD_DIM:(hh + 1) * SSD_HEAD_DIM, cols]
                ys.append(yoff[k * SSD_HEAD_DIM:(k + 1) * SSD_HEAD_DIM] * exp_r[hh:hh + 1])
                xws.append(xh * wb[hh:hh + 1])
            xw = jnp.concatenate(xws, axis=0).astype(BF16)
            sb_sc[g] = st * rows64(dec, g * HEADS_PER_GROUP) + _dot(xw, bg)
        yb_ref[0, :, cols] = jnp.concatenate(ys, axis=0)


def _ssd(xt, dtt, alog_col, dskip_col):
    b, c, l = xt.shape
    ts = TS_SSD
    ns = l // ts
    return pl.pallas_call(
        functools.partial(_ssd_kernel, nck=ts // SSD_CHUNK),
        grid=(b, ns),
        in_specs=[
            pl.BlockSpec((1, c, ts), lambda bi, i: (bi, 0, i)),
            pl.BlockSpec((1, 2 * SSD_HEADS, ts), lambda bi, i: (bi, 0, i)),
            pl.BlockSpec((1, c, ts), lambda bi, i: (bi, 0, ns - 1 - i)),
            pl.BlockSpec((1, 2 * SSD_HEADS, ts), lambda bi, i: (bi, 0, ns - 1 - i)),
            pl.BlockSpec((2 * SSD_HEADS, 1), lambda bi, i: (0, 0)),
            pl.BlockSpec((SSD_WIDTH, 1), lambda bi, i: (0, 0)),
        ],
        out_specs=[
            pl.BlockSpec((1, SSD_WIDTH, ts), lambda bi, i: (bi, 0, i)),
            pl.BlockSpec((1, SSD_WIDTH, ts), lambda bi, i: (bi, 0, ns - 1 - i)),
        ],
        out_shape=[
            jax.ShapeDtypeStruct((b, SSD_WIDTH, l), F32),
            jax.ShapeDtypeStruct((b, SSD_WIDTH, l), F32),
        ],
        scratch_shapes=[pltpu.VMEM((SSD_GROUPS, GROUP_WIDTH, D_STATE), F32),
                        pltpu.VMEM((SSD_GROUPS, GROUP_WIDTH, D_STATE), F32)],
        compiler_params=pltpu.CompilerParams(
            dimension_semantics=("parallel", "arbitrary"), vmem_limit_bytes=VMEM_LIMIT),
        name="ssd",
    )(xt, dtt, xt, dtt, alog_col, dskip_col)


def _rms(x, w):
    ms = jnp.mean(x * x, axis=-1, keepdims=True)
    return x * lax.rsqrt(ms + EPS) * w


def _out_ffn_kernel(x_ref, attn_ref, yf_ref, yb_ref, z_ref, snw_ref, wo_ref, fnw_ref,
                    wg_ref, wu_ref, wd_ref, finw_ref, o_ref, *, final, d_ff):
    x = x_ref[0]
    y = (yf_ref[0] + yb_ref[0]).T
    z = z_ref[0]
    y = y * (z * _sigmoid(z))
    yn = jnp.concatenate(
        [_rms(y[:, g * GROUP_WIDTH:(g + 1) * GROUP_WIDTH], 1.0) for g in range(SSD_GROUPS)], axis=1)
    yn = (yn * snw_ref[...]).astype(BF16)
    mixed = jnp.concatenate([attn_ref[0], yn], axis=1)
    x1 = x + _dot(mixed, wo_ref[...])
    h = _rms(x1, fnw_ref[...]).astype(BF16)
    acc = x1
    for c0, c1 in ((0, FF_SPLIT), (FF_SPLIT, d_ff)):
        gt = _dot(h, wg_ref[:, c0:c1])
        up = _dot(h, wu_ref[:, c0:c1])
        act = (gt * _sigmoid(gt) * up).astype(BF16)
        acc = acc + _dot(act, wd_ref[c0:c1, :])
    if final:
        acc = _rms(acc, finw_ref[...])
    o_ref[0] = acc


def _out_ffn(x, attn, yf, yb, z, snw, wo, fnw, wg, wu, wd, finw, final):
    b, l, d = x.shape
    d_ff = wg.shape[1]
    tm = TM_FFN
    resident = lambda shape: pl.BlockSpec(shape, lambda bi, i: (0,) * len(shape),
                                          pipeline_mode=pl.Buffered(1))
    return pl.pallas_call(
        functools.partial(_out_ffn_kernel, final=final, d_ff=d_ff),
        grid=(b, l // tm),
        in_specs=[
            pl.BlockSpec((1, tm, d), lambda bi, i: (bi, i, 0)),
            pl.BlockSpec((1, tm, ATTN_WIDTH), lambda bi, i: (bi, i, 0)),
            pl.BlockSpec((1, SSD_WIDTH, tm), lambda bi, i: (bi, 0, i)),
            pl.BlockSpec((1, SSD_WIDTH, tm), lambda bi, i: (bi, 0, i)),
            pl.BlockSpec((1, tm, SSD_WIDTH), lambda bi, i: (bi, i, 0)),
            resident((1, SSD_WIDTH)),
            resident((ATTN_WIDTH + SSD_WIDTH, d)),
            resident((1, d)),
            resident((d, d_ff)),
            resident((d, d_ff)),
            resident((d_ff, d)),
            resident((1, d)),
        ],
        out_specs=pl.BlockSpec((1, tm, d), lambda bi, i: (bi, i, 0)),
        out_shape=jax.ShapeDtypeStruct((b, l, d), F32),
        compiler_params=pltpu.CompilerParams(
            dimension_semantics=("parallel", "parallel"), vmem_limit_bytes=VMEM_LIMIT),
        name="out_ffn",
    )(x, attn, yf, yb, z, snw, wo, fnw, wg, wu, wd, finw)


def _rope_tables(seq_len):
    t = jnp.arange(seq_len)
    row = (t // GRID_W).astype(F32)
    col = (t % GRID_W).astype(F32)
    half = ROPE_AXIS_DIM // 2
    inv_freq = ROPE_THETA ** (-(2.0 * jnp.arange(half, dtype=F32)) / ROPE_AXIS_DIM)
    ang_r = row[:, None] * inv_freq[None, :]
    ang_c = col[:, None] * inv_freq[None, :]
    cos = jnp.concatenate([jnp.cos(ang_r), jnp.cos(ang_r), jnp.cos(ang_c), jnp.cos(ang_c)], axis=1)
    sin = jnp.concatenate([-jnp.sin(ang_r), jnp.sin(ang_r), -jnp.sin(ang_c), jnp.sin(ang_c)], axis=1)
    reps = LANES // HEAD_DIM
    return jnp.tile(cos, (1, reps)), jnp.tile(sin, (1, reps))


def kernel(x, norm_mix_w, w_in, q_norm_w, k_norm_w, conv_w, conv_b, dt_bias, a_log, d_skip,
           ssd_norm_w, w_out, norm_ffn_w, w_gate, w_up, w_down, final_norm_w):
    b, l, d = x.shape
    depth = w_in.shape[0]
    assert l % TM_MIX == 0 and l % TS_SSD == 0 and l % TC_CONV == 0 and l % TQ_ATTN == 0
    assert l % GRID_W == 0 and w_in.shape[2] == OFF_DT + 2 * SSD_HEADS
    cos, sin = _rope_tables(l)
    finw = final_norm_w.reshape(1, d)
    for i in range(depth):
        w_in_pad = jnp.pad(w_in[i], ((0, 0), (0, D_IN_PAD - w_in.shape[2]))).astype(BF16)
        qkw = jnp.concatenate([jnp.tile(q_norm_w[i], N_Q_HEADS) * (LOG2_E / math.sqrt(HEAD_DIM)),
                               jnp.tile(k_norm_w[i], N_KV_HEADS)]).reshape(1, QK_WIDTH)
        dtb = jnp.pad(dt_bias[i].reshape(1, 2 * SSD_HEADS), ((0, 0), (0, LANES - 2 * SSD_HEADS)))
        q, kt, v, z, xbc_raw, dt = _mix_in(x, norm_mix_w[i].reshape(1, d), w_in_pad, qkw, cos, sin, dtb)
        attn = _attention(q, kt, v)
        xt, dtt = _conv(xbc_raw, dt, conv_w[i], conv_b[i].reshape(1, CONV_CH))
        yf, yb = _ssd(xt, dtt, a_log[i].reshape(2 * SSD_HEADS, 1),
                      jnp.repeat(d_skip[i], SSD_HEAD_DIM).reshape(SSD_WIDTH, 1))
        x = _out_ffn(x, attn, yf, yb, z, ssd_norm_w[i].reshape(1, SSD_WIDTH), w_out[i].astype(BF16),
                     norm_ffn_w[i].reshape(1, d), w_gate[i].astype(BF16), w_up[i].astype(BF16),
                     w_down[i].astype(BF16), finw, final=(i == depth - 1))
    return x
```

```python
import functools
import math

import jax
import jax.numpy as jnp
from jax import lax
from jax.experimental import pallas as pl
from jax.experimental.pallas import tpu as pltpu

F32 = jnp.float32
BF16 = jnp.bfloat16

HEAD_DIM = 64
N_Q_HEADS = 8
N_KV_HEADS = 2
ATTN_WIDTH = N_Q_HEADS * HEAD_DIM
KV_WIDTH = N_KV_HEADS * HEAD_DIM
QK_WIDTH = ATTN_WIDTH + KV_WIDTH
SSD_HEADS = 8
SSD_HEAD_DIM = 64
SSD_WIDTH = SSD_HEADS * SSD_HEAD_DIM
SSD_GROUPS = 2
HEADS_PER_GROUP = SSD_HEADS // SSD_GROUPS
GROUP_WIDTH = SSD_WIDTH // SSD_GROUPS
D_STATE = 128
D_CONV = 5
CONV_PAD = D_CONV // 2
CONV_CH = SSD_WIDTH + 2 * SSD_GROUPS * D_STATE
GRID_W = 64
ROPE_THETA = 10000.0
ROPE_AXIS_DIM = HEAD_DIM // 2
EPS = 1e-6
LOG2_E = math.log2(math.e)

LANES = 128
SUBLANES = 8
VMEM_LIMIT = 56 * 1024 * 1024

OFF_QK = 0
OFF_V = QK_WIDTH
OFF_Z = OFF_V + KV_WIDTH
OFF_XBC = OFF_Z + SSD_WIDTH
OFF_DT = OFF_XBC + CONV_CH
D_IN_PAD = OFF_DT + LANES

TM_MIX = 512
TQ_ATTN = 512
TC_CONV = 512
SSD_CHUNK = 128
TS_SSD = 512
TM_FFN = 512
FF_SPLIT = 1536


def _sigmoid(x):
    return 1.0 / (1.0 + jnp.exp(-x))


def _split3(a):
    hi = a.astype(BF16)
    r1 = a - hi.astype(F32)
    mid = r1.astype(BF16)
    lo = (r1 - mid.astype(F32)).astype(BF16)
    return hi, mid, lo


def _dot(a, b):
    return jnp.dot(a, b, preferred_element_type=F32)


def _dot_exact(a, m01):
    hi, mid, lo = _split3(a)
    return _dot(hi, m01) + _dot(mid, m01) + _dot(lo, m01)


def _mix_in_kernel(x_ref, nw_ref, w_ref, qkw_ref, cos_ref, sin_ref, dtb_ref,
                   q_ref, kt_ref, v_ref, z_ref, xbc_ref, dt_ref):
    x = x_ref[0]
    ms = jnp.mean(x * x, axis=-1, keepdims=True)
    h = (x * lax.rsqrt(ms + EPS) * nw_ref[...]).astype(BF16)

    qkv = _dot(h, w_ref[:, OFF_QK:OFF_Z])
    cos = cos_ref[...]
    sin = sin_ref[...]
    qkw = qkw_ref[...]
    lane = lax.broadcasted_iota(jnp.int32, cos.shape, 1)
    low_head = lane < HEAD_DIM
    first_half = (lane & (ROPE_AXIS_DIM // 2)) == 0
    rot = []
    for c in range(QK_WIDTH // LANES):
        xc = qkv[:, c * LANES:(c + 1) * LANES]
        sq = xc * xc
        ss_lo = jnp.sum(jnp.where(low_head, sq, 0.0), axis=-1, keepdims=True)
        ss_hi = jnp.sum(jnp.where(low_head, 0.0, sq), axis=-1, keepdims=True)
        inv = jnp.where(low_head, lax.rsqrt(ss_lo * (1.0 / HEAD_DIM) + EPS),
                        lax.rsqrt(ss_hi * (1.0 / HEAD_DIM) + EPS))
        xn = xc * inv * qkw[:, c * LANES:(c + 1) * LANES]
        partner = jnp.where(first_half,
                            pltpu.roll(xn, LANES - ROPE_AXIS_DIM // 2, 1),
                            pltpu.roll(xn, ROPE_AXIS_DIM // 2, 1))
        rot.append(xn * cos + partner * sin)

    for c in range(ATTN_WIDTH // LANES):
        q_ref[0, 2 * c] = rot[c][:, :HEAD_DIM].astype(BF16)
        q_ref[0, 2 * c + 1] = rot[c][:, HEAD_DIM:].astype(BF16)
    kt = rot[ATTN_WIDTH // LANES].T
    kt_ref[0, 0, 0] = kt[:HEAD_DIM].astype(BF16)
    kt_ref[0, 1, 0] = kt[HEAD_DIM:].astype(BF16)

    vv = qkv[:, OFF_V:OFF_V + KV_WIDTH]
    ones_col = jnp.where(lane == HEAD_DIM, 1.0, 0.0)
    v_ref[0, 0] = jnp.where(low_head, vv, ones_col).astype(BF16)
    v_ref[0, 1] = jnp.where(low_head, pltpu.roll(vv, HEAD_DIM, 1), ones_col).astype(BF16)

    z_ref[0] = _dot(h, w_ref[:, OFF_Z:OFF_XBC])
    xbc_ref[0] = _dot(h, w_ref[:, OFF_XBC:OFF_DT])
    dtp = _dot(h, w_ref[:, OFF_DT:D_IN_PAD]) + dtb_ref[...]
    dt_ref[0] = jnp.maximum(dtp, 0.0) + jnp.log1p(jnp.exp(-jnp.abs(dtp)))


def _mix_in(x, nw, w_in_pad, qkw, cos, sin, dtb):
    b, l, d = x.shape
    tm = TM_MIX
    nt = l // tm
    const = lambda shape: pl.BlockSpec(shape, lambda bi, i: (0,) * len(shape))
    return pl.pallas_call(
        _mix_in_kernel,
        grid=(b, nt),
        in_specs=[
            pl.BlockSpec((1, tm, d), lambda bi, i: (bi, i, 0)),
            const((1, d)),
            const((d, D_IN_PAD)),
            const((1, QK_WIDTH)),
            pl.BlockSpec((tm, LANES), lambda bi, i: (i, 0)),
            pl.BlockSpec((tm, LANES), lambda bi, i: (i, 0)),
            const((1, LANES)),
        ],
        out_specs=[
            pl.BlockSpec((1, N_Q_HEADS, tm, HEAD_DIM), lambda bi, i: (bi, 0, i, 0)),
            pl.BlockSpec((1, N_KV_HEADS, 1, HEAD_DIM, tm), lambda bi, i: (bi, 0, i, 0, 0)),
            pl.BlockSpec((1, N_KV_HEADS, tm, LANES), lambda bi, i: (bi, 0, i, 0)),
            pl.BlockSpec((1, tm, SSD_WIDTH), lambda bi, i: (bi, i, 0)),
            pl.BlockSpec((1, tm, CONV_CH), lambda bi, i: (bi, i, 0)),
            pl.BlockSpec((1, tm, LANES), lambda bi, i: (bi, i, 0)),
        ],
        out_shape=[
            jax.ShapeDtypeStruct((b, N_Q_HEADS, l, HEAD_DIM), BF16),
            jax.ShapeDtypeStruct((b, N_KV_HEADS, nt, HEAD_DIM, tm), BF16),
            jax.ShapeDtypeStruct((b, N_KV_HEADS, l, LANES), BF16),
            jax.ShapeDtypeStruct((b, l, SSD_WIDTH), F32),
            jax.ShapeDtypeStruct((b, l, CONV_CH), F32),
            jax.ShapeDtypeStruct((b, l, LANES), F32),
        ],
        compiler_params=pltpu.CompilerParams(
            dimension_semantics=("parallel", "parallel"), vmem_limit_bytes=VMEM_LIMIT),
        name="mix_in",
    )(x, nw, w_in_pad, qkw, cos, sin, dtb)


def _attn_kernel(q_ref, kt_ref, v_ref, o_ref, *, nk, tk, tq):
    q = q_ref[0].reshape(2 * tq, HEAD_DIM)
    m = jnp.full((2 * tq, LANES), -jnp.inf, F32)
    acc = jnp.zeros((2 * tq, LANES), F32)
    for c in range(nk):
        s = _dot(q, kt_ref[0, 0, c])
        m_new = jnp.maximum(m, jnp.max(s, axis=-1, keepdims=True))
        p = jnp.exp2((s - jnp.concatenate([m_new] * (tk // LANES), axis=1)).astype(BF16))
        acc = jnp.exp2(m - m_new) * acc + _dot(p, v_ref[0, 0, c * tk:(c + 1) * tk, :])
        m = m_new
    o = acc * (1.0 / acc[:, HEAD_DIM:HEAD_DIM + 1])
    lane = lax.broadcasted_iota(jnp.int32, (tq, LANES), 1)
    o_ref[0] = jnp.where(lane < HEAD_DIM, o[:tq], pltpu.roll(o[tq:], HEAD_DIM, 1)).astype(BF16)


def _attention(q, kt, v):
    b, _, l, _ = q.shape
    nk, tk = kt.shape[2], kt.shape[4]
    tq = TQ_ATTN
    pairs_per_kv = N_Q_HEADS // N_KV_HEADS // 2
    return pl.pallas_call(
        functools.partial(_attn_kernel, nk=nk, tk=tk, tq=tq),
        grid=(b, N_Q_HEADS // 2, l // tq),
        in_specs=[
            pl.BlockSpec((1, 2, tq, HEAD_DIM), lambda bi, hp, i: (bi, hp, i, 0)),
            pl.BlockSpec((1, 1, nk, HEAD_DIM, tk), lambda bi, hp, i: (bi, hp // pairs_per_kv, 0, 0, 0)),
            pl.BlockSpec((1, 1, l, LANES), lambda bi, hp, i: (bi, hp // pairs_per_kv, 0, 0)),
        ],
        out_specs=pl.BlockSpec((1, tq, LANES), lambda bi, hp, i: (bi, i, hp)),
        out_shape=jax.ShapeDtypeStruct((b, l, ATTN_WIDTH), BF16),
        compiler_params=pltpu.CompilerParams(
            dimension_semantics=("parallel", "parallel", "parallel"), vmem_limit_bytes=VMEM_LIMIT),
        name="attn",
    )(q, kt, v)


def _conv_kernel(prev_ref, cur_ref, next_ref, dt_ref, w_ref, b_ref, xt_ref, dtt_ref, ext_sc, *, tc):
    i = pl.program_id(1)
    has_prev = (i > 0).astype(F32)
    has_next = (i < pl.num_programs(1) - 1).astype(F32)
    ext_sc[0:SUBLANES] = prev_ref[0] * has_prev
    ext_sc[SUBLANES:SUBLANES + tc] = cur_ref[0]
    ext_sc[SUBLANES + tc:2 * SUBLANES + tc] = next_ref[0] * has_next
    w = w_ref[...]
    ext = ext_sc[...]
    n_ext = tc + 2 * SUBLANES
    acc = jnp.broadcast_to(b_ref[...], (tc, CONV_CH))
    for k in range(D_CONV):
        shift = (CONV_PAD - k) % n_ext
        rolled = ext if shift == 0 else pltpu.roll(ext, shift, 0)
        acc = acc + rolled[SUBLANES:SUBLANES + tc] * w[k:k + 1]
    y = acc * _sigmoid(acc)
    xt_ref[0] = y.T
    dtt_ref[0] = dt_ref[0].T[:2 * SSD_HEADS]


def _conv(xbc_raw, dt, conv_w, conv_b):
    b, l, c = xbc_raw.shape
    tc = TC_CONV
    rb = tc // SUBLANES
    last = l // SUBLANES - 1
    return pl.pallas_call(
        functools.partial(_conv_kernel, tc=tc),
        grid=(b, l // tc),
        in_specs=[
            pl.BlockSpec((1, SUBLANES, c), lambda bi, i: (bi, jnp.maximum(i * rb - 1, 0), 0)),
            pl.BlockSpec((1, tc, c), lambda bi, i: (bi, i, 0)),
            pl.BlockSpec((1, SUBLANES, c), lambda bi, i: (bi, jnp.minimum((i + 1) * rb, last), 0)),
            pl.BlockSpec((1, tc, LANES), lambda bi, i: (bi, i, 0)),
            pl.BlockSpec((D_CONV, c), lambda bi, i: (0, 0)),
            pl.BlockSpec((1, c), lambda bi, i: (0, 0)),
        ],
        out_specs=[
            pl.BlockSpec((1, c, tc), lambda bi, i: (bi, 0, i)),
            pl.BlockSpec((1, 2 * SSD_HEADS, tc), lambda bi, i: (bi, 0, i)),
        ],
        out_shape=[
            jax.ShapeDtypeStruct((b, c, l), F32),
            jax.ShapeDtypeStruct((b, 2 * SSD_HEADS, l), F32),
        ],
        scratch_shapes=[pltpu.VMEM((tc + 2 * SUBLANES, c), F32)],
        compiler_params=pltpu.CompilerParams(
            dimension_semantics=("parallel", "parallel"), vmem_limit_bytes=VMEM_LIMIT),
        name="conv",
    )(xbc_raw, xbc_raw, xbc_raw, dt, conv_w, conv_b)


def _ssd_kernel(xf_ref, dtf_ref, xb_ref, dtb_ref, alog_ref, dskip_ref, yf_ref, yb_ref,
                sf_sc, sb_sc, *, nck):
    q = SSD_CHUNK

    @pl.when(pl.program_id(1) == 0)
    def _():
        sf_sc[...] = jnp.zeros(sf_sc.shape, F32)
        sb_sc[...] = jnp.zeros(sb_sc.shape, F32)

    a16 = -jnp.exp(alog_ref[...])
    r = lax.broadcasted_iota(jnp.int32, (q, q), 0)
    c = lax.broadcasted_iota(jnp.int32, (q, q), 1)
    fwd_mask = c >= r
    bwd_mask = c <= r
    incl_fwd = jnp.where(fwd_mask, 1.0, 0.0).astype(BF16)
    incl_rev = jnp.where(bwd_mask, 1.0, 0.0).astype(BF16)
    row16 = lax.broadcasted_iota(jnp.int32, (2 * SSD_HEADS, q), 0)
    dsk = dskip_ref[...]

    def rows64(v8, h0):
        return jnp.concatenate(
            [jnp.broadcast_to(v8[h0 + k:h0 + k + 1], (SSD_HEAD_DIM, D_STATE)) for k in range(HEADS_PER_GROUP)],
            axis=0)

    sf = [sf_sc[g] for g in range(SSD_GROUPS)]
    sb = [sb_sc[g] for g in range(SSD_GROUPS)]

    def fwd_chunk(j):
        cols = slice(j * q, (j + 1) * q)
        dt_t = dtf_ref[0, :, cols]
        ac = dt_t * a16
        cs = (_dot_exact(jnp.where(row16 < SSD_HEADS, ac, 0.0), incl_fwd)
              + _dot_exact(jnp.where(row16 >= SSD_HEADS, ac, 0.0), incl_rev))
        a_last = jnp.broadcast_to(cs[:SSD_HEADS, q - 1:q], (SSD_HEADS, q))
        exp_a = jnp.exp(cs[:SSD_HEADS])
        wf = dt_t[:SSD_HEADS] * jnp.exp(a_last - cs[:SSD_HEADS])
        dec = jnp.exp(a_last)
        ys = []
        for g in range(SSD_GROUPS):
            bt = xf_ref[0, SSD_WIDTH + g * D_STATE:SSD_WIDTH + (g + 1) * D_STATE, cols]
            ct = xf_ref[0, SSD_WIDTH + (SSD_GROUPS + g) * D_STATE:
                        SSD_WIDTH + (SSD_GROUPS + g + 1) * D_STATE, cols]
            bg = bt.T.astype(BF16)
            ctb = ct.astype(BF16)
            sc_t = _dot(bg, ctb)
            st = sf[g]
            yoff = _dot(st.astype(BF16), ctb)
            xws = []
            for k in range(HEADS_PER_GROUP):
                hh = g * HEADS_PER_GROUP + k
                xh = xf_ref[0, hh * SSD_HEAD_DIM:(hh + 1) * SSD_HEAD_DIM, cols]
                a_row = jnp.broadcast_to(cs[hh:hh + 1], (q, q))
                r_row = jnp.broadcast_to(cs[SSD_HEADS + hh:SSD_HEADS + hh + 1], (q, q))
                arg = jnp.where(fwd_mask, a_row - a_row.T, r_row - r_row.T)
                e = jnp.exp(arg)
                up = (jnp.where(fwd_mask, e, 0.0) * sc_t).astype(BF16)
                lo = (jnp.where(bwd_mask, e, 0.0) * sc_t).astype(BF16)
                lhs = jnp.concatenate([xh * dt_t[hh:hh + 1], xh * dt_t[SSD_HEADS + hh:SSD_HEADS + hh + 1]],
                                      axis=1).astype(BF16)
                yd = _dot(lhs, jnp.concatenate([up, lo], axis=0))
                ys.append(yd + yoff[k * SSD_HEAD_DIM:(k + 1) * SSD_HEAD_DIM] * exp_a[hh:hh + 1]
                          + xh * dsk[hh * SSD_HEAD_DIM:(hh + 1) * SSD_HEAD_DIM])
                xws.append(xh * wf[hh:hh + 1])
            xw = jnp.concatenate(xws, axis=0).astype(BF16)
            sf[g] = st * rows64(dec, g * HEADS_PER_GROUP) + _dot(xw, bg)
        yf_ref[0, :, cols] = jnp.concatenate(ys, axis=0)

    def bwd_chunk(j):
        cols = slice(j * q, (j + 1) * q)
        dt_b = dtb_ref[0, SSD_HEADS:, cols]
        rcs = _dot_exact(dt_b * a16[SSD_HEADS:], incl_rev)
        r_first = jnp.broadcast_to(rcs[:, 0:1], (SSD_HEADS, q))
        exp_r = jnp.exp(rcs)
        wb = dt_b * jnp.exp(r_first - rcs)
        dec = jnp.exp(r_first)
        ys = []
        for g in range(SSD_GROUPS):
            bt = xb_ref[0, SSD_WIDTH + g * D_STATE:SSD_WIDTH + (g + 1) * D_STATE, cols]
            ct = xb_ref[0, SSD_WIDTH + (SSD_GROUPS + g) * D_STATE:
                        SSD_WIDTH + (SSD_GROUPS + g + 1) * D_STATE, cols]
            bg = bt.T.astype(BF16)
            st = sb[g]
            yoff = _dot(st.astype(BF16), ct.astype(BF16))
            xws = []
            for k in range(HEADS_PER_GROUP):
                hh = g * HEADS_PER_GROUP + k
                xh = xb_ref[0, hh * SSD_HEAD_DIM:(hh + 1) * SSD_HEAD_DIM, cols]
                ys.append(yoff[k * SSD_HEAD_DIM:(k + 1) * SSD_HEAD_DIM] * exp_r[hh:hh + 1])
                xws.append(xh * wb[hh:hh + 1])
            xw = jnp.concatenate(xws, axis=0).astype(BF16)
            sb[g] = st * rows64(dec, g * HEADS_PER_GROUP) + _dot(xw, bg)
        yb_ref[0, :, cols] = jnp.concatenate(ys, axis=0)

    for j in range(nck):
        fwd_chunk(j)
        bwd_chunk(nck - 1 - j)

    for g in range(SSD_GROUPS):
        sf_sc[g] = sf[g]
        sb_sc[g] = sb[g]


def _ssd(xt, dtt, alog_col, dskip_col):
    b, c, l = xt.shape
    ts = TS_SSD
    ns = l // ts
    return pl.pallas_call(
        functools.partial(_ssd_kernel, nck=ts // SSD_CHUNK),
        grid=(b, ns),
        in_specs=[
            pl.BlockSpec((1, c, ts), lambda bi, i: (bi, 0, i)),
            pl.BlockSpec((1, 2 * SSD_HEADS, ts), lambda bi, i: (bi, 0, i)),
            pl.BlockSpec((1, c, ts), lambda bi, i: (bi, 0, ns - 1 - i)),
            pl.BlockSpec((1, 2 * SSD_HEADS, ts), lambda bi, i: (bi, 0, ns - 1 - i)),
            pl.BlockSpec((2 * SSD_HEADS, 1), lambda bi, i: (0, 0)),
            pl.BlockSpec((SSD_WIDTH, SSD_CHUNK), lambda bi, i: (0, 0)),
        ],
        out_specs=[
            pl.BlockSpec((1, SSD_WIDTH, ts), lambda bi, i: (bi, 0, i)),
            pl.BlockSpec((1, SSD_WIDTH, ts), lambda bi, i: (bi, 0, ns - 1 - i)),
        ],
        out_shape=[
            jax.ShapeDtypeStruct((b, SSD_WIDTH, l), F32),
            jax.ShapeDtypeStruct((b, SSD_WIDTH, l), F32),
        ],
        scratch_shapes=[pltpu.VMEM((SSD_GROUPS, GROUP_WIDTH, D_STATE), F32),
                        pltpu.VMEM((SSD_GROUPS, GROUP_WIDTH, D_STATE), F32)],
        compiler_params=pltpu.CompilerParams(
            dimension_semantics=("parallel", "arbitrary"), vmem_limit_bytes=VMEM_LIMIT),
        name="ssd",
    )(xt, dtt, xt, dtt, alog_col, dskip_col)


def _rms(x, w):
    ms = jnp.mean(x * x, axis=-1, keepdims=True)
    return x * lax.rsqrt(ms + EPS) * w


def _out_ffn_kernel(x_ref, attn_ref, yf_ref, yb_ref, z_ref, snw_ref, wo_ref, fnw_ref,
                    wg_ref, wu_ref, wd_ref, finw_ref, o_ref, *, final, d_ff):
    x = x_ref[0]
    y = (yf_ref[0] + yb_ref[0]).T
    z = z_ref[0]
    y = y * (z * _sigmoid(z))
    yn = jnp.concatenate(
        [_rms(y[:, g * GROUP_WIDTH:(g + 1) * GROUP_WIDTH], 1.0) for g in range(SSD_GROUPS)], axis=1)
    yn = (yn * snw_ref[...]).astype(BF16)
    mixed = jnp.concatenate([attn_ref[0], yn], axis=1)
    x1 = x + _dot(mixed, wo_ref[...])
    h = _rms(x1, fnw_ref[...]).astype(BF16)
    acc = x1
    for c0, c1 in ((0, FF_SPLIT), (FF_SPLIT, d_ff)):
        gt = _dot(h, wg_ref[:, c0:c1])
        up = _dot(h, wu_ref[:, c0:c1])
        act = (gt * _sigmoid(gt) * up).astype(BF16)
        acc = acc + _dot(act, wd_ref[c0:c1, :])
    if final:
        acc = _rms(acc, finw_ref[...])
    o_ref[0] = acc


def _out_ffn(x, attn, yf, yb, z, snw, wo, fnw, wg, wu, wd, finw, final):
    b, l, d = x.shape
    d_ff = wg.shape[1]
    tm = TM_FFN
    resident = lambda shape: pl.BlockSpec(shape, lambda bi, i: (0,) * len(shape),
                                          pipeline_mode=pl.Buffered(1))
    return pl.pallas_call(
        functools.partial(_out_ffn_kernel, final=final, d_ff=d_ff),
        grid=(b, l // tm),
        in_specs=[
            pl.BlockSpec((1, tm, d), lambda bi, i: (bi, i, 0)),
            pl.BlockSpec((1, tm, ATTN_WIDTH), lambda bi, i: (bi, i, 0)),
            pl.BlockSpec((1, SSD_WIDTH, tm), lambda bi, i: (bi, 0, i)),
            pl.BlockSpec((1, SSD_WIDTH, tm), lambda bi, i: (bi, 0, i)),
            pl.BlockSpec((1, tm, SSD_WIDTH), lambda bi, i: (bi, i, 0)),
            resident((1, SSD_WIDTH)),
            resident((ATTN_WIDTH + SSD_WIDTH, d)),
            resident((1, d)),
            resident((d, d_ff)),
            resident((d, d_ff)),
            resident((d_ff, d)),
            resident((1, d)),
        ],
        out_specs=pl.BlockSpec((1, tm, d), lambda bi, i: (bi, i, 0)),
        out_shape=jax.ShapeDtypeStruct((b, l, d), F32),
        compiler_params=pltpu.CompilerParams(
            dimension_semantics=("parallel", "parallel"), vmem_limit_bytes=VMEM_LIMIT),
        name="out_ffn",
    )(x, attn, yf, yb, z, snw, wo, fnw, wg, wu, wd, finw)


def _rope_tables(seq_len):
    t = jnp.arange(seq_len)
    row = (t // GRID_W).astype(F32)
    col = (t % GRID_W).astype(F32)
    half = ROPE_AXIS_DIM // 2
    inv_freq = ROPE_THETA ** (-(2.0 * jnp.arange(half, dtype=F32)) / ROPE_AXIS_DIM)
    ang_r = row[:, None] * inv_freq[None, :]
    ang_c = col[:, None] * inv_freq[None, :]
    cos = jnp.concatenate([jnp.cos(ang_r), jnp.cos(ang_r), jnp.cos(ang_c), jnp.cos(ang_c)], axis=1)
    sin = jnp.concatenate([-jnp.sin(ang_r), jnp.sin(ang_r), -jnp.sin(ang_c), jnp.sin(ang_c)], axis=1)
    reps = LANES // HEAD_DIM
    return jnp.tile(cos, (1, reps)), jnp.tile(sin, (1, reps))


def kernel(x, norm_mix_w, w_in, q_norm_w, k_norm_w, conv_w, conv_b, dt_bias, a_log, d_skip,
           ssd_norm_w, w_out, norm_ffn_w, w_gate, w_up, w_down, final_norm_w):
    b, l, d = x.shape
    depth = w_in.shape[0]
    assert l % TM_MIX == 0 and l % TS_SSD == 0 and l % TC_CONV == 0 and l % TQ_ATTN == 0
    assert l % GRID_W == 0 and w_in.shape[2] == OFF_DT + 2 * SSD_HEADS
    cos, sin = _rope_tables(l)
    finw = final_norm_w.reshape(1, d)
    for i in range(depth):
        w_in_pad = jnp.pad(w_in[i], ((0, 0), (0, D_IN_PAD - w_in.shape[2]))).astype(BF16)
        qkw = jnp.concatenate([jnp.tile(q_norm_w[i], N_Q_HEADS) * (LOG2_E / math.sqrt(HEAD_DIM)),
                               jnp.tile(k_norm_w[i], N_KV_HEADS)]).reshape(1, QK_WIDTH)
        dtb = jnp.pad(dt_bias[i].reshape(1, 2 * SSD_HEADS), ((0, 0), (0, LANES - 2 * SSD_HEADS)))
        q, kt, v, z, xbc_raw, dt = _mix_in(x, norm_mix_w[i].reshape(1, d), w_in_pad, qkw, cos, sin, dtb)
        attn = _attention(q, kt, v)
        xt, dtt = _conv(xbc_raw, dt, conv_w[i], conv_b[i].reshape(1, CONV_CH))
        yf, yb = _ssd(xt, dtt, a_log[i].reshape(2 * SSD_HEADS, 1),
                      jnp.broadcast_to(jnp.repeat(d_skip[i], SSD_HEAD_DIM)[:, None], (SSD_WIDTH, SSD_CHUNK)))
        x = _out_ffn(x, attn, yf, yb, z, ssd_norm_w[i].reshape(1, SSD_WIDTH), w_out[i].astype(BF16),
                     norm_ffn_w[i].reshape(1, d), w_gate[i].astype(BF16), w_up[i].astype(BF16),
                     w_down[i].astype(BF16), finw, final=(i == depth - 1))
    return x
```

```python
import functools
import math

import jax
import jax.numpy as jnp
from jax import lax
from jax.experimental import pallas as pl
from jax.experimental.pallas import tpu as pltpu

F32 = jnp.float32
BF16 = jnp.bfloat16

HEAD_DIM = 64
N_Q_HEADS = 8
N_KV_HEADS = 2
ATTN_WIDTH = N_Q_HEADS * HEAD_DIM
KV_WIDTH = N_KV_HEADS * HEAD_DIM
QK_WIDTH = ATTN_WIDTH + KV_WIDTH
SSD_HEADS = 8
SSD_HEAD_DIM = 64
SSD_WIDTH = SSD_HEADS * SSD_HEAD_DIM
SSD_GROUPS = 2
HEADS_PER_GROUP = SSD_HEADS // SSD_GROUPS
GROUP_WIDTH = SSD_WIDTH // SSD_GROUPS
D_STATE = 128
D_CONV = 5
CONV_PAD = D_CONV // 2
CONV_CH = SSD_WIDTH + 2 * SSD_GROUPS * D_STATE
GRID_W = 64
ROPE_THETA = 10000.0
ROPE_AXIS_DIM = HEAD_DIM // 2
EPS = 1e-6
LOG2_E = math.log2(math.e)

LANES = 128
SUBLANES = 8
VMEM_LIMIT = 56 * 1024 * 1024

OFF_QK = 0
OFF_V = QK_WIDTH
OFF_Z = OFF_V + KV_WIDTH
OFF_XBC = OFF_Z + SSD_WIDTH
OFF_DT = OFF_XBC + CONV_CH
D_IN_PAD = OFF_DT + LANES

TM_MIX = 512
TQ_ATTN = 512
SSD_CHUNK = 128
TS_SSD = 1024
TM_FFN = 512
FF_SPLIT = 1536


def _sigmoid(x):
    return 1.0 / (1.0 + jnp.exp(-x))


def _split3(a):
    hi = a.astype(BF16)
    r1 = a - hi.astype(F32)
    mid = r1.astype(BF16)
    lo = (r1 - mid.astype(F32)).astype(BF16)
    return hi, mid, lo


def _dot(a, b):
    return jnp.dot(a, b, preferred_element_type=F32)


def _dot_exact(a, m01):
    hi, mid, lo = _split3(a)
    return _dot(hi, m01) + _dot(mid, m01) + _dot(lo, m01)


def _mix_in_kernel(xp_ref, x_ref, xn_ref, nw_ref, w_ref, qkw_ref, cos_ref, sin_ref, dtb_ref, cw_ref, cb_ref,
                   q_ref, kt_ref, v_ref, z_ref, xt_ref, dtt_ref, *, tm):
    xe = jnp.concatenate([xp_ref[0], x_ref[0], xn_ref[0]], axis=0)
    ms = jnp.mean(xe * xe, axis=-1, keepdims=True)
    he = (xe * lax.rsqrt(ms + EPS) * nw_ref[...]).astype(BF16)
    h = he[SUBLANES:SUBLANES + tm]

    qkv = _dot(h, w_ref[:, OFF_QK:OFF_Z])
    cos = cos_ref[...]
    sin = sin_ref[...]
    qkw = qkw_ref[...]
    lane = lax.broadcasted_iota(jnp.int32, cos.shape, 1)
    low_head = lane < HEAD_DIM
    first_half = (lane & (ROPE_AXIS_DIM // 2)) == 0
    rot = []
    for c in range(QK_WIDTH // LANES):
        xc = qkv[:, c * LANES:(c + 1) * LANES]
        sq = xc * xc
        ss_lo = jnp.sum(jnp.where(low_head, sq, 0.0), axis=-1, keepdims=True)
        ss_hi = jnp.sum(jnp.where(low_head, 0.0, sq), axis=-1, keepdims=True)
        inv = jnp.where(low_head, lax.rsqrt(ss_lo * (1.0 / HEAD_DIM) + EPS),
                        lax.rsqrt(ss_hi * (1.0 / HEAD_DIM) + EPS))
        xn = xc * inv * qkw[:, c * LANES:(c + 1) * LANES]
        partner = jnp.where(first_half,
                            pltpu.roll(xn, LANES - ROPE_AXIS_DIM // 2, 1),
                            pltpu.roll(xn, ROPE_AXIS_DIM // 2, 1))
        rot.append(xn * cos + partner * sin)

    for c in range(ATTN_WIDTH // LANES):
        q_ref[0, 2 * c] = rot[c][:, :HEAD_DIM].astype(BF16)
        q_ref[0, 2 * c + 1] = rot[c][:, HEAD_DIM:].astype(BF16)
    kt = rot[ATTN_WIDTH // LANES].T
    kt_ref[0, 0, 0] = kt[:HEAD_DIM].astype(BF16)
    kt_ref[0, 1, 0] = kt[HEAD_DIM:].astype(BF16)

    vv = qkv[:, OFF_V:OFF_V + KV_WIDTH]
    ones_col = jnp.where(lane == HEAD_DIM, 1.0, 0.0)
    v_ref[0, 0] = jnp.where(low_head, vv, ones_col).astype(BF16)
    v_ref[0, 1] = jnp.where(low_head, pltpu.roll(vv, HEAD_DIM, 1), ones_col).astype(BF16)

    z_ref[0] = _dot(h, w_ref[:, OFF_Z:OFF_XBC])
    dtp = _dot(h, w_ref[:, OFF_DT:D_IN_PAD]) + dtb_ref[...]
    dt = jnp.maximum(dtp, 0.0) + jnp.log1p(jnp.exp(-jnp.abs(dtp)))
    dtt_ref[0] = dt.T[:2 * SSD_HEADS]

    i = pl.program_id(1)
    xbc = _dot(he, w_ref[:, OFF_XBC:OFF_DT])
    n_ext = tm + 2 * SUBLANES
    has_prev = (i > 0).astype(F32)
    has_next = (i < pl.num_programs(1) - 1).astype(F32)
    xbc = jnp.concatenate([xbc[:SUBLANES] * has_prev, xbc[SUBLANES:SUBLANES + tm],
                           xbc[SUBLANES + tm:] * has_next], axis=0)
    cw = cw_ref[...]
    acc = jnp.broadcast_to(cb_ref[...], (tm, CONV_CH))
    for k in range(D_CONV):
        shift = (CONV_PAD - k) % n_ext
        rolled = xbc if shift == 0 else pltpu.roll(xbc, shift, 0)
        acc = acc + rolled[SUBLANES:SUBLANES + tm] * cw[k:k + 1]
    xt_ref[0] = (acc * _sigmoid(acc)).T


def _mix_in(x, nw, w_in_pad, qkw, cos, sin, dtb, conv_w, conv_b):
    b, l, d = x.shape
    tm = TM_MIX
    nt = l // tm
    rb = tm // SUBLANES
    last = l // SUBLANES - 1
    const = lambda shape: pl.BlockSpec(shape, lambda bi, i: (0,) * len(shape))
    return pl.pallas_call(
        functools.partial(_mix_in_kernel, tm=tm),
        grid=(b, nt),
        in_specs=[
            pl.BlockSpec((1, SUBLANES, d), lambda bi, i: (bi, jnp.maximum(i * rb - 1, 0), 0)),
            pl.BlockSpec((1, tm, d), lambda bi, i: (bi, i, 0)),
            pl.BlockSpec((1, SUBLANES, d), lambda bi, i: (bi, jnp.minimum((i + 1) * rb, last), 0)),
            const((1, d)),
            const((d, D_IN_PAD)),
            const((1, QK_WIDTH)),
            pl.BlockSpec((tm, LANES), lambda bi, i: (i, 0)),
            pl.BlockSpec((tm, LANES), lambda bi, i: (i, 0)),
            const((1, LANES)),
            const((D_CONV, CONV_CH)),
            const((1, CONV_CH)),
        ],
        out_specs=[
            pl.BlockSpec((1, N_Q_HEADS, tm, HEAD_DIM), lambda bi, i: (bi, 0, i, 0)),
            pl.BlockSpec((1, N_KV_HEADS, 1, HEAD_DIM, tm), lambda bi, i: (bi, 0, i, 0, 0)),
            pl.BlockSpec((1, N_KV_HEADS, tm, LANES), lambda bi, i: (bi, 0, i, 0)),
            pl.BlockSpec((1, tm, SSD_WIDTH), lambda bi, i: (bi, i, 0)),
            pl.BlockSpec((1, CONV_CH, tm), lambda bi, i: (bi, 0, i)),
            pl.BlockSpec((1, 2 * SSD_HEADS, tm), lambda bi, i: (bi, 0, i)),
        ],
        out_shape=[
            jax.ShapeDtypeStruct((b, N_Q_HEADS, l, HEAD_DIM), BF16),
            jax.ShapeDtypeStruct((b, N_KV_HEADS, nt, HEAD_DIM, tm), BF16),
            jax.ShapeDtypeStruct((b, N_KV_HEADS, l, LANES), BF16),
            jax.ShapeDtypeStruct((b, l, SSD_WIDTH), F32),
            jax.ShapeDtypeStruct((b, CONV_CH, l), F32),
            jax.ShapeDtypeStruct((b, 2 * SSD_HEADS, l), F32),
        ],
        compiler_params=pltpu.CompilerParams(
            dimension_semantics=("parallel", "parallel"), vmem_limit_bytes=VMEM_LIMIT),
        name="mix_in",
    )(x, x, x, nw, w_in_pad, qkw, cos, sin, dtb, conv_w, conv_b)


def _attn_kernel(q_ref, kt_ref, v_ref, o_ref, *, nk, tk, tq):
    q = q_ref[0].reshape(2 * tq, HEAD_DIM)
    m = jnp.full((2 * tq, LANES), -jnp.inf, F32)
    acc = jnp.zeros((2 * tq, LANES), F32)
    for c in range(nk):
        s = _dot(q, kt_ref[0, 0, c])
        m_new = jnp.maximum(m, jnp.max(s, axis=-1, keepdims=True))
        p = jnp.exp2((s - jnp.concatenate([m_new] * (tk // LANES), axis=1)).astype(BF16))
        acc = jnp.exp2(m - m_new) * acc + _dot(p, v_ref[0, 0, c * tk:(c + 1) * tk, :])
        m = m_new
    o = acc * (1.0 / acc[:, HEAD_DIM:HEAD_DIM + 1])
    lane = lax.broadcasted_iota(jnp.int32, (tq, LANES), 1)
    o_ref[0] = jnp.where(lane < HEAD_DIM, o[:tq], pltpu.roll(o[tq:], HEAD_DIM, 1)).astype(BF16)


def _attention(q, kt, v):
    b, _, l, _ = q.shape
    nk, tk = kt.shape[2], kt.shape[4]
    tq = TQ_ATTN
    pairs_per_kv = N_Q_HEADS // N_KV_HEADS // 2
    return pl.pallas_call(
        functools.partial(_attn_kernel, nk=nk, tk=tk, tq=tq),
        grid=(b, N_Q_HEADS // 2, l // tq),
        in_specs=[
            pl.BlockSpec((1, 2, tq, HEAD_DIM), lambda bi, hp, i: (bi, hp, i, 0)),
            pl.BlockSpec((1, 1, nk, HEAD_DIM, tk), lambda bi, hp, i: (bi, hp // pairs_per_kv, 0, 0, 0)),
            pl.BlockSpec((1, 1, l, LANES), lambda bi, hp, i: (bi, hp // pairs_per_kv, 0, 0)),
        ],
        out_specs=pl.BlockSpec((1, tq, LANES), lambda bi, hp, i: (bi, i, hp)),
        out_shape=jax.ShapeDtypeStruct((b, l, ATTN_WIDTH), BF16),
        compiler_params=pltpu.CompilerParams(
            dimension_semantics=("parallel", "parallel", "parallel"), vmem_limit_bytes=VMEM_LIMIT),
        name="attn",
    )(q, kt, v)


def _ssd_kernel(xf_ref, dtf_ref, xb_ref, dtb_ref, alog_ref, dskip_ref, yf_ref, yb_ref,
                sf_sc, sb_sc, *, nck):
    q = SSD_CHUNK

    @pl.when(pl.program_id(1) == 0)
    def _():
        sf_sc[...] = jnp.zeros(sf_sc.shape, F32)
        sb_sc[...] = jnp.zeros(sb_sc.shape, F32)

    a16 = -jnp.exp(alog_ref[...])
    r = lax.broadcasted_iota(jnp.int32, (q, q), 0)
    c = lax.broadcasted_iota(jnp.int32, (q, q), 1)
    fwd_mask = c >= r
    bwd_mask = c <= r
    incl_fwd = jnp.where(fwd_mask, 1.0, 0.0).astype(BF16)
    incl_rev = jnp.where(bwd_mask, 1.0, 0.0).astype(BF16)
    row16 = lax.broadcasted_iota(jnp.int32, (2 * SSD_HEADS, q), 0)
    dsk = dskip_ref[...]

    def rows64(v8, h0):
        return jnp.concatenate(
            [jnp.broadcast_to(v8[h0 + k:h0 + k + 1], (SSD_HEAD_DIM, D_STATE)) for k in range(HEADS_PER_GROUP)],
            axis=0)

    sf = [sf_sc[g] for g in range(SSD_GROUPS)]
    sb = [sb_sc[g] for g in range(SSD_GROUPS)]

    def fwd_prep(j):
        cols = slice(j * q, (j + 1) * q)
        dt_t = dtf_ref[0, :, cols]
        ac = dt_t * a16
        cs = (_dot_exact(jnp.where(row16 < SSD_HEADS, ac, 0.0), incl_fwd)
              + _dot_exact(jnp.where(row16 >= SSD_HEADS, ac, 0.0), incl_rev))
        a_last = jnp.broadcast_to(cs[:SSD_HEADS, q - 1:q], (SSD_HEADS, q))
        exp_a = jnp.exp(cs[:SSD_HEADS])
        wf = dt_t[:SSD_HEADS] * jnp.exp(a_last - cs[:SSD_HEADS])
        dec = jnp.exp(a_last)
        groups = []
        for g in range(SSD_GROUPS):
            bt = xf_ref[0, SSD_WIDTH + g * D_STATE:SSD_WIDTH + (g + 1) * D_STATE, cols]
            ct = xf_ref[0, SSD_WIDTH + (SSD_GROUPS + g) * D_STATE:
                        SSD_WIDTH + (SSD_GROUPS + g + 1) * D_STATE, cols]
            bg = bt.T.astype(BF16)
            ctb = ct.astype(BF16)
            groups.append((bg, ctb, _dot(bg, ctb)))
        col_forms = [jnp.broadcast_to(cs[k:k + 1], (q, q)).T for k in range(2 * SSD_HEADS)]
        return cols, dt_t, cs, exp_a, wf, dec, groups, col_forms

    def fwd_main(prep):
        cols, dt_t, cs, exp_a, wf, dec, groups, col_forms = prep
        ys = []
        for g in range(SSD_GROUPS):
            bg, ctb, sc_t = groups[g]
            st = sf[g]
            yoff = _dot(st.astype(BF16), ctb)
            xws = []
            for k in range(HEADS_PER_GROUP):
                hh = g * HEADS_PER_GROUP + k
                xh = xf_ref[0, hh * SSD_HEAD_DIM:(hh + 1) * SSD_HEAD_DIM, cols]
                arg = jnp.where(fwd_mask, cs[hh:hh + 1] - col_forms[hh],
                                cs[SSD_HEADS + hh:SSD_HEADS + hh + 1] - col_forms[SSD_HEADS + hh])
                e = jnp.exp(arg)
                up = (jnp.where(fwd_mask, e, 0.0) * sc_t).astype(BF16)
                lo = (jnp.where(bwd_mask, e, 0.0) * sc_t).astype(BF16)
                lhs = jnp.concatenate([xh * dt_t[hh:hh + 1], xh * dt_t[SSD_HEADS + hh:SSD_HEADS + hh + 1]],
                                      axis=1).astype(BF16)
                yd = _dot(lhs, jnp.concatenate([up, lo], axis=0))
                ys.append(yd + yoff[k * SSD_HEAD_DIM:(k + 1) * SSD_HEAD_DIM] * exp_a[hh:hh + 1]
                          + xh * dsk[hh * SSD_HEAD_DIM:(hh + 1) * SSD_HEAD_DIM])
                xws.append(xh * wf[hh:hh + 1])
            xw = jnp.concatenate(xws, axis=0).astype(BF16)
            sf[g] = st * rows64(dec, g * HEADS_PER_GROUP) + _dot(xw, bg)
        yf_ref[0, :, cols] = jnp.concatenate(ys, axis=0)

    def bwd_prep(j):
        cols = slice(j * q, (j + 1) * q)
        dt_b = dtb_ref[0, SSD_HEADS:, cols]
        rcs = _dot_exact(dt_b * a16[SSD_HEADS:], incl_rev)
        r_first = jnp.broadcast_to(rcs[:, 0:1], (SSD_HEADS, q))
        exp_r = jnp.exp(rcs)
        wb = dt_b * jnp.exp(r_first - rcs)
        dec = jnp.exp(r_first)
        groups = []
        for g in range(SSD_GROUPS):
            bt = xb_ref[0, SSD_WIDTH + g * D_STATE:SSD_WIDTH + (g + 1) * D_STATE, cols]
            ct = xb_ref[0, SSD_WIDTH + (SSD_GROUPS + g) * D_STATE:
                        SSD_WIDTH + (SSD_GROUPS + g + 1) * D_STATE, cols]
            groups.append((bt.T.astype(BF16), ct.astype(BF16)))
        return cols, exp_r, wb, dec, groups

    def bwd_main(prep):
        cols, exp_r, wb, dec, groups = prep
        ys = []
        for g in range(SSD_GROUPS):
            bg, ctb = groups[g]
            st = sb[g]
            yoff = _dot(st.astype(BF16), ctb)
            xws = []
            for k in range(HEADS_PER_GROUP):
                hh = g * HEADS_PER_GROUP + k
                xh = xb_ref[0, hh * SSD_HEAD_DIM:(hh + 1) * SSD_HEAD_DIM, cols]
                ys.append(yoff[k * SSD_HEAD_DIM:(k + 1) * SSD_HEAD_DIM] * exp_r[hh:hh + 1])
                xws.append(xh * wb[hh:hh + 1])
            xw = jnp.concatenate(xws, axis=0).astype(BF16)
            sb[g] = st * rows64(dec, g * HEADS_PER_GROUP) + _dot(xw, bg)
        yb_ref[0, :, cols] = jnp.concatenate(ys, axis=0)

    preps = [fwd_prep(j) for j in range(nck)]
    for j in range(nck):
        fwd_main(preps[j])
        bwd_main(bwd_prep(nck - 1 - j))

    for g in range(SSD_GROUPS):
        sf_sc[g] = sf[g]
        sb_sc[g] = sb[g]


def _ssd(xt, dtt, alog_col, dskip_col):
    b, c, l = xt.shape
    ts = TS_SSD
    ns = l // ts
    return pl.pallas_call(
        functools.partial(_ssd_kernel, nck=ts // SSD_CHUNK),
        grid=(b, ns),
        in_specs=[
            pl.BlockSpec((1, c, ts), lambda bi, i: (bi, 0, i)),
            pl.BlockSpec((1, 2 * SSD_HEADS, ts), lambda bi, i: (bi, 0, i)),
            pl.BlockSpec((1, c, ts), lambda bi, i: (bi, 0, ns - 1 - i)),
            pl.BlockSpec((1, 2 * SSD_HEADS, ts), lambda bi, i: (bi, 0, ns - 1 - i)),
            pl.BlockSpec((2 * SSD_HEADS, 1), lambda bi, i: (0, 0)),
            pl.BlockSpec((SSD_WIDTH, SSD_CHUNK), lambda bi, i: (0, 0)),
        ],
        out_specs=[
            pl.BlockSpec((1, SSD_WIDTH, ts), lambda bi, i: (bi, 0, i)),
            pl.BlockSpec((1, SSD_WIDTH, ts), lambda bi, i: (bi, 0, ns - 1 - i)),
        ],
        out_shape=[
            jax.ShapeDtypeStruct((b, SSD_WIDTH, l), F32),
            jax.ShapeDtypeStruct((b, SSD_WIDTH, l), F32),
        ],
        scratch_shapes=[pltpu.VMEM((SSD_GROUPS, GROUP_WIDTH, D_STATE), F32),
                        pltpu.VMEM((SSD_GROUPS, GROUP_WIDTH, D_STATE), F32)],
        compiler_params=pltpu.CompilerParams(
            dimension_semantics=("parallel", "arbitrary"), vmem_limit_bytes=VMEM_LIMIT),
        name="ssd",
    )(xt, dtt, xt, dtt, alog_col, dskip_col)


def _rms(x, w):
    ms = jnp.mean(x * x, axis=-1, keepdims=True)
    return x * lax.rsqrt(ms + EPS) * w


def _out_ffn_kernel(x_ref, attn_ref, yf_ref, yb_ref, z_ref, snw_ref, wo_ref, fnw_ref,
                    wg_ref, wu_ref, wd_ref, finw_ref, o_ref, *, final, d_ff):
    x = x_ref[0]
    y = (yf_ref[0] + yb_ref[0]).T
    z = z_ref[0]
    y = y * (z * _sigmoid(z))
    yn = jnp.concatenate(
        [_rms(y[:, g * GROUP_WIDTH:(g + 1) * GROUP_WIDTH], 1.0) for g in range(SSD_GROUPS)], axis=1)
    yn = (yn * snw_ref[...]).astype(BF16)
    mixed = jnp.concatenate([attn_ref[0], yn], axis=1)
    x1 = x + _dot(mixed, wo_ref[...])
    h = _rms(x1, fnw_ref[...]).astype(BF16)
    acc = x1
    for c0, c1 in ((0, FF_SPLIT), (FF_SPLIT, d_ff)):
        gt = _dot(h, wg_ref[:, c0:c1])
        up = _dot(h, wu_ref[:, c0:c1])
        act = (gt * _sigmoid(gt) * up).astype(BF16)
        acc = acc + _dot(act, wd_ref[c0:c1, :])
    if final:
        acc = _rms(acc, finw_ref[...])
    o_ref[0] = acc


def _out_ffn(x, attn, yf, yb, z, snw, wo, fnw, wg, wu, wd, finw, final):
    b, l, d = x.shape
    d_ff = wg.shape[1]
    tm = TM_FFN
    resident = lambda shape: pl.BlockSpec(shape, lambda bi, i: (0,) * len(shape),
                                          pipeline_mode=pl.Buffered(1))
    return pl.pallas_call(
        functools.partial(_out_ffn_kernel, final=final, d_ff=d_ff),
        grid=(b, l // tm),
        in_specs=[
            pl.BlockSpec((1, tm, d), lambda bi, i: (bi, i, 0)),
            pl.BlockSpec((1, tm, ATTN_WIDTH), lambda bi, i: (bi, i, 0)),
            pl.BlockSpec((1, SSD_WIDTH, tm), lambda bi, i: (bi, 0, i)),
            pl.BlockSpec((1, SSD_WIDTH, tm), lambda bi, i: (bi, 0, i)),
            pl.BlockSpec((1, tm, SSD_WIDTH), lambda bi, i: (bi, i, 0)),
            resident((1, SSD_WIDTH)),
            resident((ATTN_WIDTH + SSD_WIDTH, d)),
            resident((1, d)),
            resident((d, d_ff)),
            resident((d, d_ff)),
            resident((d_ff, d)),
            resident((1, d)),
        ],
        out_specs=pl.BlockSpec((1, tm, d), lambda bi, i: (bi, i, 0)),
        out_shape=jax.ShapeDtypeStruct((b, l, d), F32),
        compiler_params=pltpu.CompilerParams(
            dimension_semantics=("parallel", "parallel"), vmem_limit_bytes=VMEM_LIMIT),
        name="out_ffn",
    )(x, attn, yf, yb, z, snw, wo, fnw, wg, wu, wd, finw)


def _rope_tables(seq_len):
    t = jnp.arange(seq_len)
    row = (t // GRID_W).astype(F32)
    col = (t % GRID_W).astype(F32)
    half = ROPE_AXIS_DIM // 2
    inv_freq = ROPE_THETA ** (-(2.0 * jnp.arange(half, dtype=F32)) / ROPE_AXIS_DIM)
    ang_r = row[:, None] * inv_freq[None, :]
    ang_c = col[:, None] * inv_freq[None, :]
    cos = jnp.concatenate([jnp.cos(ang_r), jnp.cos(ang_r), jnp.cos(ang_c), jnp.cos(ang_c)], axis=1)
    sin = jnp.concatenate([-jnp.sin(ang_r), jnp.sin(ang_r), -jnp.sin(ang_c), jnp.sin(ang_c)], axis=1)
    reps = LANES // HEAD_DIM
    return jnp.tile(cos, (1, reps)), jnp.tile(sin, (1, reps))


def kernel(x, norm_mix_w, w_in, q_norm_w, k_norm_w, conv_w, conv_b, dt_bias, a_log, d_skip,
           ssd_norm_w, w_out, norm_ffn_w, w_gate, w_up, w_down, final_norm_w):
    b, l, d = x.shape
    depth = w_in.shape[0]
    assert l % TM_MIX == 0 and l % TS_SSD == 0 and l % TQ_ATTN == 0 and l % TM_FFN == 0
    assert l % GRID_W == 0 and w_in.shape[2] == OFF_DT + 2 * SSD_HEADS
    cos, sin = _rope_tables(l)
    finw = final_norm_w.reshape(1, d)
    for i in range(depth):
        w_in_pad = jnp.pad(w_in[i], ((0, 0), (0, D_IN_PAD - w_in.shape[2]))).astype(BF16)
        qkw = jnp.concatenate([jnp.tile(q_norm_w[i], N_Q_HEADS) * (LOG2_E / math.sqrt(HEAD_DIM)),
                               jnp.tile(k_norm_w[i], N_KV_HEADS)]).reshape(1, QK_WIDTH)
        dtb = jnp.pad(dt_bias[i].reshape(1, 2 * SSD_HEADS), ((0, 0), (0, LANES - 2 * SSD_HEADS)))
        q, kt, v, z, xt, dtt = _mix_in(x, norm_mix_w[i].reshape(1, d), w_in_pad, qkw, cos, sin, dtb,
                                       conv_w[i], conv_b[i].reshape(1, CONV_CH))
        attn = _attention(q, kt, v)
        yf, yb = _ssd(xt, dtt, a_log[i].reshape(2 * SSD_HEADS, 1),
                      jnp.broadcast_to(jnp.repeat(d_skip[i], SSD_HEAD_DIM)[:, None], (SSD_WIDTH, SSD_CHUNK)))
        x = _out_ffn(x, attn, yf, yb, z, ssd_norm_w[i].reshape(1, SSD_WIDTH), w_out[i].astype(BF16),
                     norm_ffn_w[i].reshape(1, d), w_gate[i].astype(BF16), w_up[i].astype(BF16),
                     w_down[i].astype(BF16), finw, final=(i == depth - 1))
    return x
```

```python
import functools
import math

import jax
import jax.numpy as jnp
from jax import lax
from jax.experimental import pallas as pl
from jax.experimental.pallas import tpu as pltpu

F32 = jnp.float32
BF16 = jnp.bfloat16

HEAD_DIM = 64
N_Q_HEADS = 8
N_KV_HEADS = 2
ATTN_WIDTH = N_Q_HEADS * HEAD_DIM
KV_WIDTH = N_KV_HEADS * HEAD_DIM
QK_WIDTH = ATTN_WIDTH + KV_WIDTH
SSD_HEADS = 8
SSD_HEAD_DIM = 64
SSD_WIDTH = SSD_HEADS * SSD_HEAD_DIM
SSD_GROUPS = 2
HEADS_PER_GROUP = SSD_HEADS // SSD_GROUPS
GROUP_WIDTH = SSD_WIDTH // SSD_GROUPS
D_STATE = 128
D_CONV = 5
CONV_PAD = D_CONV // 2
CONV_CH = SSD_WIDTH + 2 * SSD_GROUPS * D_STATE
GRID_W = 64
ROPE_THETA = 10000.0
ROPE_AXIS_DIM = HEAD_DIM // 2
EPS = 1e-6
LOG2_E = math.log2(math.e)

LANES = 128
SUBLANES = 8
VMEM_LIMIT = 56 * 1024 * 1024

OFF_QK = 0
OFF_V = QK_WIDTH
OFF_Z = OFF_V + KV_WIDTH
OFF_XBC = OFF_Z + SSD_WIDTH
OFF_DT = OFF_XBC + CONV_CH
D_IN_PAD = OFF_DT + LANES

TM_MIX = 512
TQ_ATTN = 512
SUBTILES_ATTN = 4
TK_ATTN = 512
SSD_CHUNK = 128
TS_SSD = 1024
TM_FFN = 512
FF_SPLIT = 1536

def _sigmoid(x):
    return 1.0 / (1.0 + jnp.exp(-x))


def _split3(a):
    hi = a.astype(BF16)
    r1 = a - hi.astype(F32)
    mid = r1.astype(BF16)
    lo = (r1 - mid.astype(F32)).astype(BF16)
    return hi, mid, lo


def _dot(a, b):
    return jnp.dot(a, b, preferred_element_type=F32)


def _dot_exact(a, m01):
    hi, mid, lo = _split3(a)
    return _dot(hi, m01) + _dot(mid, m01) + _dot(lo, m01)


def _mix_in_kernel(xp_ref, x_ref, xn_ref, nw_ref, w_ref, qkw_ref, cos_ref, sin_ref, dtb_ref, cw_ref, cb_ref,
                   q_ref, kt_ref, v_ref, z_ref, xt_ref, dtt_ref, *, tm):
    xe = jnp.concatenate([xp_ref[0], x_ref[0], xn_ref[0]], axis=0)
    ms = jnp.mean(xe * xe, axis=-1, keepdims=True)
    he = (xe * lax.rsqrt(ms + EPS) * nw_ref[...]).astype(BF16)
    h = he[SUBLANES:SUBLANES + tm]

    conv_tile = 2 * LANES
    qkv = _dot(h, w_ref[:, OFF_QK:OFF_Z])
    xbc_tiles = [_dot(he, w_ref[:, OFF_XBC + c0:OFF_XBC + c0 + conv_tile])
                 for c0 in range(0, CONV_CH, conv_tile)]
    z_proj = _dot(h, w_ref[:, OFF_Z:OFF_XBC])
    dtp = _dot(h, w_ref[:, OFF_DT:D_IN_PAD]) + dtb_ref[...]

    i = pl.program_id(1)
    n_ext = tm + 2 * SUBLANES
    has_prev = (i > 0).astype(F32)
    has_next = (i < pl.num_programs(1) - 1).astype(F32)
    cw = cw_ref[...]
    cb = cb_ref[...]

    def conv_epilogue(t):
        ch = slice(t * conv_tile, (t + 1) * conv_tile)
        xbc = xbc_tiles[t]
        xbc = jnp.concatenate([xbc[:SUBLANES] * has_prev, xbc[SUBLANES:SUBLANES + tm],
                               xbc[SUBLANES + tm:] * has_next], axis=0)
        acc = jnp.broadcast_to(cb[:, ch], (tm, conv_tile))
        for k in range(D_CONV):
            shift = (CONV_PAD - k) % n_ext
            rolled = xbc if shift == 0 else pltpu.roll(xbc, shift, 0)
            acc = acc + rolled[SUBLANES:SUBLANES + tm] * cw[k:k + 1, ch]
        xt_ref[0, ch, :] = (acc * _sigmoid(acc)).T

    cos = cos_ref[...]
    sin = sin_ref[...]
    qkw = qkw_ref[...]
    lane = lax.broadcasted_iota(jnp.int32, cos.shape, 1)
    low_head = lane < HEAD_DIM
    first_half = (lane & (ROPE_AXIS_DIM // 2)) == 0

    def qk_epilogue(c):
        xc = qkv[:, c * LANES:(c + 1) * LANES]
        sq = xc * xc
        ss_lo = jnp.sum(jnp.where(low_head, sq, 0.0), axis=-1, keepdims=True)
        ss_hi = jnp.sum(jnp.where(low_head, 0.0, sq), axis=-1, keepdims=True)
        inv = jnp.where(low_head, lax.rsqrt(ss_lo * (1.0 / HEAD_DIM) + EPS),
                        lax.rsqrt(ss_hi * (1.0 / HEAD_DIM) + EPS))
        xn = xc * inv * qkw[:, c * LANES:(c + 1) * LANES]
        partner = jnp.where(first_half,
                            pltpu.roll(xn, LANES - ROPE_AXIS_DIM // 2, 1),
                            pltpu.roll(xn, ROPE_AXIS_DIM // 2, 1))
        rot = xn * cos + partner * sin
        if c < ATTN_WIDTH // LANES:
            q_ref[0, 2 * c] = rot[:, :HEAD_DIM].astype(BF16)
            q_ref[0, 2 * c + 1] = rot[:, HEAD_DIM:].astype(BF16)
        else:
            kt = rot.T
            kt_ref[0, 0, 0] = kt[:HEAD_DIM].astype(BF16)
            kt_ref[0, 1, 0] = kt[HEAD_DIM:].astype(BF16)

    def v_epilogue():
        vv = qkv[:, OFF_V:OFF_V + KV_WIDTH]
        ones_col = jnp.where(lane == HEAD_DIM, 1.0, 0.0)
        v_ref[0, 0] = jnp.where(low_head, vv, ones_col).astype(BF16)
        v_ref[0, 1] = jnp.where(low_head, pltpu.roll(vv, HEAD_DIM, 1), ones_col).astype(BF16)

    for c in range(QK_WIDTH // LANES):
        qk_epilogue(c)
    v_epilogue()
    for t in range(CONV_CH // conv_tile):
        conv_epilogue(t)

    z_ref[0] = z_proj
    dt = jnp.maximum(dtp, 0.0) + jnp.log1p(jnp.exp(-jnp.abs(dtp)))
    dtt_ref[0] = dt.T[:2 * SSD_HEADS]


def _mix_in(x, nw, w_in_pad, qkw, cos, sin, dtb, conv_w, conv_b):
    b, l, d = x.shape
    tm = TM_MIX
    nt = l // tm
    rb = tm // SUBLANES
    last = l // SUBLANES - 1
    const = lambda shape: pl.BlockSpec(shape, lambda bi, i: (0,) * len(shape))
    return pl.pallas_call(
        functools.partial(_mix_in_kernel, tm=tm),
        grid=(b, nt),
        in_specs=[
            pl.BlockSpec((1, SUBLANES, d), lambda bi, i: (bi, jnp.maximum(i * rb - 1, 0), 0)),
            pl.BlockSpec((1, tm, d), lambda bi, i: (bi, i, 0)),
            pl.BlockSpec((1, SUBLANES, d), lambda bi, i: (bi, jnp.minimum((i + 1) * rb, last), 0)),
            const((1, d)),
            const((d, D_IN_PAD)),
            const((1, QK_WIDTH)),
            pl.BlockSpec((tm, LANES), lambda bi, i: (i, 0)),
            pl.BlockSpec((tm, LANES), lambda bi, i: (i, 0)),
            const((1, LANES)),
            const((D_CONV, CONV_CH)),
            const((1, CONV_CH)),
        ],
        out_specs=[
            pl.BlockSpec((1, N_Q_HEADS, tm, HEAD_DIM), lambda bi, i: (bi, 0, i, 0)),
            pl.BlockSpec((1, N_KV_HEADS, 1, HEAD_DIM, tm), lambda bi, i: (bi, 0, i, 0, 0)),
            pl.BlockSpec((1, N_KV_HEADS, tm, LANES), lambda bi, i: (bi, 0, i, 0)),
            pl.BlockSpec((1, tm, SSD_WIDTH), lambda bi, i: (bi, i, 0)),
            pl.BlockSpec((1, CONV_CH, tm), lambda bi, i: (bi, 0, i)),
            pl.BlockSpec((1, 2 * SSD_HEADS, tm), lambda bi, i: (bi, 0, i)),
        ],
        out_shape=[
            jax.ShapeDtypeStruct((b, N_Q_HEADS, l, HEAD_DIM), BF16),
            jax.ShapeDtypeStruct((b, N_KV_HEADS, nt, HEAD_DIM, tm), BF16),
            jax.ShapeDtypeStruct((b, N_KV_HEADS, l, LANES), BF16),
            jax.ShapeDtypeStruct((b, l, SSD_WIDTH), F32),
            jax.ShapeDtypeStruct((b, CONV_CH, l), F32),
            jax.ShapeDtypeStruct((b, 2 * SSD_HEADS, l), F32),
        ],
        compiler_params=pltpu.CompilerParams(
            dimension_semantics=("parallel", "parallel"), vmem_limit_bytes=VMEM_LIMIT),
        name="mix_in",
    )(x, x, x, nw, w_in_pad, qkw, cos, sin, dtb, conv_w, conv_b)


def _attn_kernel(q_ref, kt_ref, v_ref, o_ref, *, nk, tk, tq, n_sub):
    per_block = kt_ref.shape[4] // tk
    lane = lax.broadcasted_iota(jnp.int32, (tq, LANES), 1)

    def scores(q, c):
        kt = kt_ref[0, 0, c // per_block, :, (c % per_block) * tk:(c % per_block + 1) * tk]
        return _dot(q, kt)

    def softmax(s, m):
        m_new = jnp.maximum(m, jnp.max(s, axis=-1, keepdims=True))
        p = jnp.exp2((s - jnp.concatenate([m_new] * (tk // LANES), axis=1)).astype(BF16))
        return p, m_new

    def accumulate(acc, p, m, m_new, c):
        return jnp.exp2(m - m_new) * acc + _dot(p, v_ref[0, 0, c * tk:(c + 1) * tk, :])

    def finish(acc, u):
        o = acc * (1.0 / acc[:, HEAD_DIM:HEAD_DIM + 1])
        o_ref[0, u * tq:(u + 1) * tq, :] = jnp.where(
            lane < HEAD_DIM, o[:tq], pltpu.roll(o[tq:], HEAD_DIM, 1)).astype(BF16)

    pending = None
    for u in range(n_sub):
        q = q_ref[0, :, u * tq:(u + 1) * tq, :].reshape(2 * tq, HEAD_DIM)
        m = jnp.full((2 * tq, LANES), -jnp.inf, F32)
        acc = jnp.zeros((2 * tq, LANES), F32)
        s = scores(q, 0)
        if pending is not None:
            p_acc, p_p, p_m, p_mn = pending
            p0, m0 = softmax(s, m)
            finish(accumulate(p_acc, p_p, p_m, p_mn, nk - 1), u - 1)
            acc = accumulate(acc, p0, m, m0, 0)
            m = m0
            first = 1
        else:
            first = 0
        for c in range(first, nk):
            if c > 0:
                s = scores(q, c)
            p, m_new = softmax(s, m)
            if c == nk - 1 and u < n_sub - 1:
                pending = (acc, p, m, m_new)
            else:
                acc = accumulate(acc, p, m, m_new, c)
                m = m_new
    finish(acc, n_sub - 1)


def _attention(q, kt, v):
    b, _, l, _ = q.shape
    n_blk, t_blk = kt.shape[2], kt.shape[4]
    tk = TK_ATTN
    nk = l // tk
    tq = TQ_ATTN
    n_sub = SUBTILES_ATTN
    tq_step = n_sub * tq
    pairs_per_kv = N_Q_HEADS // N_KV_HEADS // 2
    return pl.pallas_call(
        functools.partial(_attn_kernel, nk=nk, tk=tk, tq=tq, n_sub=n_sub),
        grid=(b, N_Q_HEADS // 2, l // tq_step),
        in_specs=[
            pl.BlockSpec((1, 2, tq_step, HEAD_DIM), lambda bi, hp, i: (bi, hp, i, 0)),
            pl.BlockSpec((1, 1, n_blk, HEAD_DIM, t_blk), lambda bi, hp, i: (bi, hp // pairs_per_kv, 0, 0, 0)),
            pl.BlockSpec((1, 1, l, LANES), lambda bi, hp, i: (bi, hp // pairs_per_kv, 0, 0)),
        ],
        out_specs=pl.BlockSpec((1, tq_step, LANES), lambda bi, hp, i: (bi, i, hp)),
        out_shape=jax.ShapeDtypeStruct((b, l, ATTN_WIDTH), BF16),
        compiler_params=pltpu.CompilerParams(
            dimension_semantics=("parallel", "parallel", "parallel"), vmem_limit_bytes=VMEM_LIMIT),
        name="attn",
    )(q, kt, v)


def _ssd_kernel(xf_ref, dtf_ref, xb_ref, dtb_ref, alog_ref, dskip_ref, yf_ref, yb_ref,
                sf_sc, sb_sc, *, nck):
    q = SSD_CHUNK

    @pl.when(pl.program_id(1) == 0)
    def _():
        sf_sc[...] = jnp.zeros(sf_sc.shape, F32)
        sb_sc[...] = jnp.zeros(sb_sc.shape, F32)

    a16 = -jnp.exp(alog_ref[...]) * LOG2_E
    r = lax.broadcasted_iota(jnp.int32, (q, q), 0)
    c = lax.broadcasted_iota(jnp.int32, (q, q), 1)
    fwd_mask = c >= r
    bwd_mask = c <= r
    incl_fwd = jnp.where(fwd_mask, 1.0, 0.0).astype(BF16)
    incl_rev = jnp.where(bwd_mask, 1.0, 0.0).astype(BF16)
    row16 = lax.broadcasted_iota(jnp.int32, (2 * SSD_HEADS, q), 0)
    dsk = dskip_ref[...]

    def rows64(v8, h0):
        return jnp.concatenate(
            [jnp.broadcast_to(v8[h0 + k:h0 + k + 1], (SSD_HEAD_DIM, D_STATE)) for k in range(HEADS_PER_GROUP)],
            axis=0)

    sf = [sf_sc[g] for g in range(SSD_GROUPS)]
    sb = [sb_sc[g] for g in range(SSD_GROUPS)]

    def fwd_prep(j):
        cols = slice(j * q, (j + 1) * q)
        dt_t = dtf_ref[0, :, cols]
        ac = dt_t * a16
        cs = (_dot_exact(jnp.where(row16 < SSD_HEADS, ac, 0.0), incl_fwd)
              + _dot_exact(jnp.where(row16 >= SSD_HEADS, ac, 0.0), incl_rev))
        a_last = jnp.broadcast_to(cs[:SSD_HEADS, q - 1:q], (SSD_HEADS, q))
        exp_a = jnp.exp2(cs[:SSD_HEADS])
        wf = dt_t[:SSD_HEADS] * jnp.exp2(a_last - cs[:SSD_HEADS])
        dec = jnp.exp2(a_last)
        groups = []
        for g in range(SSD_GROUPS):
            bt = xf_ref[0, SSD_WIDTH + g * D_STATE:SSD_WIDTH + (g + 1) * D_STATE, cols]
            ct = xf_ref[0, SSD_WIDTH + (SSD_GROUPS + g) * D_STATE:
                        SSD_WIDTH + (SSD_GROUPS + g + 1) * D_STATE, cols]
            bg = bt.T.astype(BF16)
            ctb = ct.astype(BF16)
            sc_t = _dot(bg, ctb)
            groups.append((bg, ctb, jnp.where(fwd_mask, sc_t, 0.0), jnp.where(bwd_mask, sc_t, 0.0)))
        col_forms = [jnp.broadcast_to(cs[k:k + 1], (q, q)).T for k in range(2 * SSD_HEADS)]
        return cols, dt_t, cs, exp_a, wf, dec, groups, col_forms

    def fwd_main(prep):
        cols, dt_t, cs, exp_a, wf, dec, groups, col_forms = prep
        ys = []
        for g in range(SSD_GROUPS):
            bg, ctb, sc_fwd, sc_bwd = groups[g]
            st = sf[g]
            yoff = _dot(st.astype(BF16), ctb)
            xws = []
            for k in range(HEADS_PER_GROUP):
                hh = g * HEADS_PER_GROUP + k
                xh = xf_ref[0, hh * SSD_HEAD_DIM:(hh + 1) * SSD_HEAD_DIM, cols]
                arg = jnp.where(fwd_mask, cs[hh:hh + 1] - col_forms[hh],
                                cs[SSD_HEADS + hh:SSD_HEADS + hh + 1] - col_forms[SSD_HEADS + hh])
                e = jnp.exp2(arg)
                up = (e * sc_fwd).astype(BF16)
                lo = (e * sc_bwd).astype(BF16)
                lhs = jnp.concatenate([xh * dt_t[hh:hh + 1], xh * dt_t[SSD_HEADS + hh:SSD_HEADS + hh + 1]],
                                      axis=1).astype(BF16)
                yd = _dot(lhs, jnp.concatenate([up, lo], axis=0))
                ys.append(yd + yoff[k * SSD_HEAD_DIM:(k + 1) * SSD_HEAD_DIM] * exp_a[hh:hh + 1]
                          + xh * dsk[hh * SSD_HEAD_DIM:(hh + 1) * SSD_HEAD_DIM])
                xws.append(xh * wf[hh:hh + 1])
            xw = jnp.concatenate(xws, axis=0).astype(BF16)
            sf[g] = st * rows64(dec, g * HEADS_PER_GROUP) + _dot(xw, bg)
        yf_ref[0, :, cols] = jnp.concatenate(ys, axis=0)

    def bwd_prep(j):
        cols = slice(j * q, (j + 1) * q)
        dt_b = dtb_ref[0, SSD_HEADS:, cols]
        rcs = _dot_exact(dt_b * a16[SSD_HEADS:], incl_rev)
        r_first = jnp.broadcast_to(rcs[:, 0:1], (SSD_HEADS, q))
        exp_r = jnp.exp2(rcs)
        wb = dt_b * jnp.exp2(r_first - rcs)
        dec = jnp.exp2(r_first)
        groups = []
        for g in range(SSD_GROUPS):
            bt = xb_ref[0, SSD_WIDTH + g * D_STATE:SSD_WIDTH + (g + 1) * D_STATE, cols]
            ct = xb_ref[0, SSD_WIDTH + (SSD_GROUPS + g) * D_STATE:
                        SSD_WIDTH + (SSD_GROUPS + g + 1) * D_STATE, cols]
            groups.append((bt.T.astype(BF16), ct.astype(BF16)))
        return cols, exp_r, wb, dec, groups

    def bwd_main(prep):
        cols, exp_r, wb, dec, groups = prep
        ys = []
        for g in range(SSD_GROUPS):
            bg, ctb = groups[g]
            st = sb[g]
            yoff = _dot(st.astype(BF16), ctb)
            xws = []
            for k in range(HEADS_PER_GROUP):
                hh = g * HEADS_PER_GROUP + k
                xh = xb_ref[0, hh * SSD_HEAD_DIM:(hh + 1) * SSD_HEAD_DIM, cols]
                ys.append(yoff[k * SSD_HEAD_DIM:(k + 1) * SSD_HEAD_DIM] * exp_r[hh:hh + 1])
                xws.append(xh * wb[hh:hh + 1])
            xw = jnp.concatenate(xws, axis=0).astype(BF16)
            sb[g] = st * rows64(dec, g * HEADS_PER_GROUP) + _dot(xw, bg)
        yb_ref[0, :, cols] = jnp.concatenate(ys, axis=0)

    preps = [fwd_prep(j) for j in range(nck)]
    for j in range(nck):
        fwd_main(preps[j])
        bwd_main(bwd_prep(nck - 1 - j))

    for g in range(SSD_GROUPS):
        sf_sc[g] = sf[g]
        sb_sc[g] = sb[g]


def _ssd(xt, dtt, alog_col, dskip_col):
    b, c, l = xt.shape
    ts = TS_SSD
    ns = l // ts
    return pl.pallas_call(
        functools.partial(_ssd_kernel, nck=ts // SSD_CHUNK),
        grid=(b, ns),
        in_specs=[
            pl.BlockSpec((1, c, ts), lambda bi, i: (bi, 0, i)),
            pl.BlockSpec((1, 2 * SSD_HEADS, ts), lambda bi, i: (bi, 0, i)),
            pl.BlockSpec((1, c, ts), lambda bi, i: (bi, 0, ns - 1 - i)),
            pl.BlockSpec((1, 2 * SSD_HEADS, ts), lambda bi, i: (bi, 0, ns - 1 - i)),
            pl.BlockSpec((2 * SSD_HEADS, 1), lambda bi, i: (0, 0)),
            pl.BlockSpec((SSD_WIDTH, SSD_CHUNK), lambda bi, i: (0, 0)),
        ],
        out_specs=[
            pl.BlockSpec((1, SSD_WIDTH, ts), lambda bi, i: (bi, 0, i)),
            pl.BlockSpec((1, SSD_WIDTH, ts), lambda bi, i: (bi, 0, ns - 1 - i)),
        ],
        out_shape=[
            jax.ShapeDtypeStruct((b, SSD_WIDTH, l), F32),
            jax.ShapeDtypeStruct((b, SSD_WIDTH, l), F32),
        ],
        scratch_shapes=[pltpu.VMEM((SSD_GROUPS, GROUP_WIDTH, D_STATE), F32),
                        pltpu.VMEM((SSD_GROUPS, GROUP_WIDTH, D_STATE), F32)],
        compiler_params=pltpu.CompilerParams(
            dimension_semantics=("parallel", "arbitrary"), vmem_limit_bytes=VMEM_LIMIT),
        name="ssd",
    )(xt, dtt, xt, dtt, alog_col, dskip_col)


def _rms(x, w):
    ms = jnp.mean(x * x, axis=-1, keepdims=True)
    return x * lax.rsqrt(ms + EPS) * w


def _out_ffn_kernel(x_ref, attn_ref, yf_ref, yb_ref, z_ref, snw_ref, wo_ref, fnw_ref,
                    wg_ref, wu_ref, wd_ref, finw_ref, o_ref, *, final, d_ff):
    x = x_ref[0]
    y = (yf_ref[0] + yb_ref[0]).T
    z = z_ref[0]
    y = y * (z * _sigmoid(z))
    yn = jnp.concatenate(
        [_rms(y[:, g * GROUP_WIDTH:(g + 1) * GROUP_WIDTH], 1.0) for g in range(SSD_GROUPS)], axis=1)
    yn = (yn * snw_ref[...]).astype(BF16)
    mixed = jnp.concatenate([attn_ref[0], yn], axis=1)
    x1 = x + _dot(mixed, wo_ref[...])
    h = _rms(x1, fnw_ref[...]).astype(BF16)
    acc = x1
    for c0, c1 in ((0, FF_SPLIT), (FF_SPLIT, d_ff)):
        gt = _dot(h, wg_ref[:, c0:c1])
        up = _dot(h, wu_ref[:, c0:c1])
        act = (gt * _sigmoid(gt) * up).astype(BF16)
        acc = acc + _dot(act, wd_ref[c0:c1, :])
    if final:
        acc = _rms(acc, finw_ref[...])
    o_ref[0] = acc


def _out_ffn(x, attn, yf, yb, z, snw, wo, fnw, wg, wu, wd, finw, final):
    b, l, d = x.shape
    d_ff = wg.shape[1]
    tm = TM_FFN
    resident = lambda shape: pl.BlockSpec(shape, lambda bi, i: (0,) * len(shape),
                                          pipeline_mode=pl.Buffered(1))
    return pl.pallas_call(
        functools.partial(_out_ffn_kernel, final=final, d_ff=d_ff),
        grid=(b, l // tm),
        in_specs=[
            pl.BlockSpec((1, tm, d), lambda bi, i: (bi, i, 0)),
            pl.BlockSpec((1, tm, ATTN_WIDTH), lambda bi, i: (bi, i, 0)),
            pl.BlockSpec((1, SSD_WIDTH, tm), lambda bi, i: (bi, 0, i)),
            pl.BlockSpec((1, SSD_WIDTH, tm), lambda bi, i: (bi, 0, i)),
            pl.BlockSpec((1, tm, SSD_WIDTH), lambda bi, i: (bi, i, 0)),
            resident((1, SSD_WIDTH)),
            resident((ATTN_WIDTH + SSD_WIDTH, d)),
            resident((1, d)),
            resident((d, d_ff)),
            resident((d, d_ff)),
            resident((d_ff, d)),
            resident((1, d)),
        ],
        out_specs=pl.BlockSpec((1, tm, d), lambda bi, i: (bi, i, 0)),
        out_shape=jax.ShapeDtypeStruct((b, l, d), F32),
        compiler_params=pltpu.CompilerParams(
            dimension_semantics=("parallel", "parallel"), vmem_limit_bytes=VMEM_LIMIT),
        name="out_ffn",
    )(x, attn, yf, yb, z, snw, wo, fnw, wg, wu, wd, finw)


def _rope_tables(seq_len):
    t = jnp.arange(seq_len)
    row = (t // GRID_W).astype(F32)
    col = (t % GRID_W).astype(F32)
    half = ROPE_AXIS_DIM // 2
    inv_freq = ROPE_THETA ** (-(2.0 * jnp.arange(half, dtype=F32)) / ROPE_AXIS_DIM)
    ang_r = row[:, None] * inv_freq[None, :]
    ang_c = col[:, None] * inv_freq[None, :]
    cos = jnp.concatenate([jnp.cos(ang_r), jnp.cos(ang_r), jnp.cos(ang_c), jnp.cos(ang_c)], axis=1)
    sin = jnp.concatenate([-jnp.sin(ang_r), jnp.sin(ang_r), -jnp.sin(ang_c), jnp.sin(ang_c)], axis=1)
    reps = LANES // HEAD_DIM
    return jnp.tile(cos, (1, reps)), jnp.tile(sin, (1, reps))


def kernel(x, norm_mix_w, w_in, q_norm_w, k_norm_w, conv_w, conv_b, dt_bias, a_log, d_skip,
           ssd_norm_w, w_out, norm_ffn_w, w_gate, w_up, w_down, final_norm_w):
    b, l, d = x.shape
    depth = w_in.shape[0]
    assert l % TM_MIX == 0 and l % TS_SSD == 0 and l % (SUBTILES_ATTN * TQ_ATTN) == 0 and l % TM_FFN == 0
    assert TM_MIX % TK_ATTN == 0
    assert l % GRID_W == 0 and w_in.shape[2] == OFF_DT + 2 * SSD_HEADS
    cos, sin = _rope_tables(l)
    finw = final_norm_w.reshape(1, d)
    for i in range(depth):
        w_in_pad = jnp.pad(w_in[i], ((0, 0), (0, D_IN_PAD - w_in.shape[2]))).astype(BF16)
        qkw = jnp.concatenate([jnp.tile(q_norm_w[i], N_Q_HEADS) * (LOG2_E / math.sqrt(HEAD_DIM)),
                               jnp.tile(k_norm_w[i], N_KV_HEADS)]).reshape(1, QK_WIDTH)
        dtb = jnp.pad(dt_bias[i].reshape(1, 2 * SSD_HEADS), ((0, 0), (0, LANES - 2 * SSD_HEADS)))
        q, kt, v, z, xt, dtt = _mix_in(x, norm_mix_w[i].reshape(1, d), w_in_pad, qkw, cos, sin, dtb,
                                       conv_w[i], conv_b[i].reshape(1, CONV_CH))
        attn = _attention(q, kt, v)
        yf, yb = _ssd(xt, dtt, a_log[i].reshape(2 * SSD_HEADS, 1),
                      jnp.broadcast_to(jnp.repeat(d_skip[i], SSD_HEAD_DIM)[:, None], (SSD_WIDTH, SSD_CHUNK)))
        x = _out_ffn(x, attn, yf, yb, z, ssd_norm_w[i].reshape(1, SSD_WIDTH), w_out[i].astype(BF16),
                     norm_ffn_w[i].reshape(1, d), w_gate[i].astype(BF16), w_up[i].astype(BF16),
                     w_down[i].astype(BF16), finw, final=(i == depth - 1))
    return x
```

```python
import functools
import math

import jax
import jax.numpy as jnp
from jax import lax
from jax.experimental import pallas as pl
from jax.experimental.pallas import tpu as pltpu

F32 = jnp.float32
BF16 = jnp.bfloat16

HEAD_DIM = 64
N_Q_HEADS = 8
N_KV_HEADS = 2
ATTN_WIDTH = N_Q_HEADS * HEAD_DIM
KV_WIDTH = N_KV_HEADS * HEAD_DIM
QK_WIDTH = ATTN_WIDTH + KV_WIDTH
SSD_HEADS = 8
SSD_HEAD_DIM = 64
SSD_WIDTH = SSD_HEADS * SSD_HEAD_DIM
SSD_GROUPS = 2
HEADS_PER_GROUP = SSD_HEADS // SSD_GROUPS
GROUP_WIDTH = SSD_WIDTH // SSD_GROUPS
D_STATE = 128
D_CONV = 5
CONV_PAD = D_CONV // 2
CONV_CH = SSD_WIDTH + 2 * SSD_GROUPS * D_STATE
GRID_W = 64
ROPE_THETA = 10000.0
ROPE_AXIS_DIM = HEAD_DIM // 2
EPS = 1e-6
LOG2_E = math.log2(math.e)

LANES = 128
SUBLANES = 8
VMEM_LIMIT = 56 * 1024 * 1024

OFF_QK = 0
OFF_V = QK_WIDTH
OFF_Z = OFF_V + KV_WIDTH
OFF_XBC = OFF_Z + SSD_WIDTH
OFF_DT = OFF_XBC + CONV_CH
D_IN_PAD = OFF_DT + LANES

TM_MIX = 512
SUBTILES_MIX = 2
TQ_ATTN = 512
SUBTILES_ATTN = 4
TK_ATTN = 512
SSD_CHUNK = 128
TS_SSD = 1024
TM_FFN = 512
SUBTILES_FFN = 2
FF_SPLIT = 1536

def _sigmoid(x):
    return 1.0 / (1.0 + jnp.exp(-x))


def _split3(a):
    hi = a.astype(BF16)
    r1 = a - hi.astype(F32)
    mid = r1.astype(BF16)
    lo = (r1 - mid.astype(F32)).astype(BF16)
    return hi, mid, lo


def _dot(a, b):
    return jnp.dot(a, b, preferred_element_type=F32)


def _dot_exact(a, m01):
    hi, mid, lo = _split3(a)
    return _dot(hi, m01) + _dot(mid, m01) + _dot(lo, m01)


def _mix_in_kernel(xp_ref, x_ref, xn_ref, nw_ref, w_ref, qkw_ref, cos_ref, sin_ref, dtb_ref, cw_ref, cb_ref,
                   q_ref, kt_ref, v_ref, z_ref, xt_ref, dtt_ref, *, tm, n_sub):
    i = pl.program_id(1)
    n_ext = tm + 2 * SUBLANES
    conv_tile = 2 * LANES
    cw = cw_ref[...]
    cb = cb_ref[...]
    qkw = qkw_ref[...]
    lane = lax.broadcasted_iota(jnp.int32, (tm, LANES), 1)
    low_head = lane < HEAD_DIM
    first_half = (lane & (ROPE_AXIS_DIM // 2)) == 0

    def project(u):
        before = xp_ref[0] if u == 0 else x_ref[0, u * tm - SUBLANES:u * tm]
        after = xn_ref[0] if u == n_sub - 1 else x_ref[0, (u + 1) * tm:(u + 1) * tm + SUBLANES]
        xe = jnp.concatenate([before, x_ref[0, u * tm:(u + 1) * tm], after], axis=0)
        ms = jnp.mean(xe * xe, axis=-1, keepdims=True)
        he = (xe * lax.rsqrt(ms + EPS) * nw_ref[...]).astype(BF16)
        h = he[SUBLANES:SUBLANES + tm]
        qkv = _dot(h, w_ref[:, OFF_QK:OFF_Z])
        xbc_tiles = [_dot(he, w_ref[:, OFF_XBC + c0:OFF_XBC + c0 + conv_tile])
                     for c0 in range(0, CONV_CH, conv_tile)]
        z_proj = _dot(h, w_ref[:, OFF_Z:OFF_XBC])
        dtp = _dot(h, w_ref[:, OFF_DT:D_IN_PAD]) + dtb_ref[...]
        return qkv, xbc_tiles, z_proj, dtp

    projections = [project(u) for u in range(n_sub)]
    for u in range(n_sub):
        _mix_in_epilogue(u, projections[u], i, cw, cb, qkw, lane, low_head, first_half, cos_ref, sin_ref,
                         q_ref, kt_ref, v_ref, z_ref, xt_ref, dtt_ref, tm=tm, n_sub=n_sub)


def _mix_in_epilogue(u, projection, i, cw, cb, qkw, lane, low_head, first_half, cos_ref, sin_ref,
                     q_ref, kt_ref, v_ref, z_ref, xt_ref, dtt_ref, *, tm, n_sub):
    qkv, xbc_tiles, z_proj, dtp = projection
    rows = slice(u * tm, (u + 1) * tm)
    n_ext = tm + 2 * SUBLANES
    conv_tile = 2 * LANES
    has_prev = (i > 0).astype(F32) if u == 0 else None
    has_next = (i < pl.num_programs(1) - 1).astype(F32) if u == n_sub - 1 else None

    def conv_epilogue(t):
        ch = slice(t * conv_tile, (t + 1) * conv_tile)
        xbc = xbc_tiles[t]
        head = xbc[:SUBLANES] if has_prev is None else xbc[:SUBLANES] * has_prev
        tail = xbc[SUBLANES + tm:] if has_next is None else xbc[SUBLANES + tm:] * has_next
        xbc = jnp.concatenate([head, xbc[SUBLANES:SUBLANES + tm], tail], axis=0)
        acc = jnp.broadcast_to(cb[:, ch], (tm, conv_tile))
        for k in range(D_CONV):
            shift = (CONV_PAD - k) % n_ext
            rolled = xbc if shift == 0 else pltpu.roll(xbc, shift, 0)
            acc = acc + rolled[SUBLANES:SUBLANES + tm] * cw[k:k + 1, ch]
        xt_ref[0, ch, rows] = (acc * _sigmoid(acc)).T

    cos = cos_ref[rows, :]
    sin = sin_ref[rows, :]

    def qk_epilogue(c):
        xc = qkv[:, c * LANES:(c + 1) * LANES]
        sq = xc * xc
        ss_lo = jnp.sum(jnp.where(low_head, sq, 0.0), axis=-1, keepdims=True)
        ss_hi = jnp.sum(jnp.where(low_head, 0.0, sq), axis=-1, keepdims=True)
        inv = jnp.where(low_head, lax.rsqrt(ss_lo * (1.0 / HEAD_DIM) + EPS),
                        lax.rsqrt(ss_hi * (1.0 / HEAD_DIM) + EPS))
        xn = xc * inv * qkw[:, c * LANES:(c + 1) * LANES]
        partner = jnp.where(first_half,
                            pltpu.roll(xn, LANES - ROPE_AXIS_DIM // 2, 1),
                            pltpu.roll(xn, ROPE_AXIS_DIM // 2, 1))
        rot = xn * cos + partner * sin
        if c < ATTN_WIDTH // LANES:
            q_ref[0, 2 * c, rows] = rot[:, :HEAD_DIM].astype(BF16)
            q_ref[0, 2 * c + 1, rows] = rot[:, HEAD_DIM:].astype(BF16)
        else:
            kt = rot.T
            kt_ref[0, 0, u] = kt[:HEAD_DIM].astype(BF16)
            kt_ref[0, 1, u] = kt[HEAD_DIM:].astype(BF16)

    def v_epilogue():
        vv = qkv[:, OFF_V:OFF_V + KV_WIDTH]
        ones_col = jnp.where(lane == HEAD_DIM, 1.0, 0.0)
        v_ref[0, 0, rows] = jnp.where(low_head, vv, ones_col).astype(BF16)
        v_ref[0, 1, rows] = jnp.where(low_head, pltpu.roll(vv, HEAD_DIM, 1), ones_col).astype(BF16)

    for c in range(QK_WIDTH // LANES):
        qk_epilogue(c)
    v_epilogue()
    for t in range(CONV_CH // conv_tile):
        conv_epilogue(t)

    z_ref[0, rows] = z_proj
    dt = jnp.maximum(dtp, 0.0) + jnp.log1p(jnp.exp(-jnp.abs(dtp)))
    dtt_ref[0, :, rows] = dt.T[:2 * SSD_HEADS]


def _mix_in(x, nw, w_in_pad, qkw, cos, sin, dtb, conv_w, conv_b):
    b, l, d = x.shape
    tm = TM_MIX
    n_sub = SUBTILES_MIX
    ts = n_sub * tm
    nt = l // tm
    rb = ts // SUBLANES
    last = l // SUBLANES - 1
    const = lambda shape: pl.BlockSpec(shape, lambda bi, i: (0,) * len(shape))
    return pl.pallas_call(
        functools.partial(_mix_in_kernel, tm=tm, n_sub=n_sub),
        grid=(b, l // ts),
        in_specs=[
            pl.BlockSpec((1, SUBLANES, d), lambda bi, i: (bi, jnp.maximum(i * rb - 1, 0), 0)),
            pl.BlockSpec((1, ts, d), lambda bi, i: (bi, i, 0)),
            pl.BlockSpec((1, SUBLANES, d), lambda bi, i: (bi, jnp.minimum((i + 1) * rb, last), 0)),
            const((1, d)),
            const((d, D_IN_PAD)),
            const((1, QK_WIDTH)),
            pl.BlockSpec((ts, LANES), lambda bi, i: (i, 0)),
            pl.BlockSpec((ts, LANES), lambda bi, i: (i, 0)),
            const((1, LANES)),
            const((D_CONV, CONV_CH)),
            const((1, CONV_CH)),
        ],
        out_specs=[
            pl.BlockSpec((1, N_Q_HEADS, ts, HEAD_DIM), lambda bi, i: (bi, 0, i, 0)),
            pl.BlockSpec((1, N_KV_HEADS, n_sub, HEAD_DIM, tm), lambda bi, i: (bi, 0, i, 0, 0)),
            pl.BlockSpec((1, N_KV_HEADS, ts, LANES), lambda bi, i: (bi, 0, i, 0)),
            pl.BlockSpec((1, ts, SSD_WIDTH), lambda bi, i: (bi, i, 0)),
            pl.BlockSpec((1, CONV_CH, ts), lambda bi, i: (bi, 0, i)),
            pl.BlockSpec((1, 2 * SSD_HEADS, ts), lambda bi, i: (bi, 0, i)),
        ],
        out_shape=[
            jax.ShapeDtypeStruct((b, N_Q_HEADS, l, HEAD_DIM), BF16),
            jax.ShapeDtypeStruct((b, N_KV_HEADS, nt, HEAD_DIM, tm), BF16),
            jax.ShapeDtypeStruct((b, N_KV_HEADS, l, LANES), BF16),
            jax.ShapeDtypeStruct((b, l, SSD_WIDTH), F32),
            jax.ShapeDtypeStruct((b, CONV_CH, l), F32),
            jax.ShapeDtypeStruct((b, 2 * SSD_HEADS, l), F32),
        ],
        compiler_params=pltpu.CompilerParams(
            dimension_semantics=("parallel", "parallel"), vmem_limit_bytes=VMEM_LIMIT),
        name="mix_in",
    )(x, x, x, nw, w_in_pad, qkw, cos, sin, dtb, conv_w, conv_b)


def _attn_kernel(q_ref, kt_ref, v_ref, o_ref, *, nk, tk, tq, n_sub):
    per_block = kt_ref.shape[4] // tk
    lane = lax.broadcasted_iota(jnp.int32, (tq, LANES), 1)

    def scores(q, c):
        kt = kt_ref[0, 0, c // per_block, :, (c % per_block) * tk:(c % per_block + 1) * tk]
        return _dot(q, kt)

    def softmax(s, m):
        m_new = jnp.maximum(m, jnp.max(s, axis=-1, keepdims=True))
        p = jnp.exp2((s - jnp.concatenate([m_new] * (tk // LANES), axis=1)).astype(BF16))
        return p, m_new

    def accumulate(acc, p, m, m_new, c):
        return jnp.exp2(m - m_new) * acc + _dot(p, v_ref[0, 0, c * tk:(c + 1) * tk, :])

    def finish(acc, u):
        o = acc * (1.0 / acc[:, HEAD_DIM:HEAD_DIM + 1])
        o_ref[0, u * tq:(u + 1) * tq, :] = jnp.where(
            lane < HEAD_DIM, o[:tq], pltpu.roll(o[tq:], HEAD_DIM, 1)).astype(BF16)

    pending = None
    for u in range(n_sub):
        q = q_ref[0, :, u * tq:(u + 1) * tq, :].reshape(2 * tq, HEAD_DIM)
        m = jnp.full((2 * tq, LANES), -jnp.inf, F32)
        acc = jnp.zeros((2 * tq, LANES), F32)
        s = scores(q, 0)
        if pending is not None:
            p_acc, p_p, p_m, p_mn = pending
            p0, m0 = softmax(s, m)
            finish(accumulate(p_acc, p_p, p_m, p_mn, nk - 1), u - 1)
            acc = accumulate(acc, p0, m, m0, 0)
            m = m0
            first = 1
        else:
            first = 0
        for c in range(first, nk):
            if c > 0:
                s = scores(q, c)
            p, m_new = softmax(s, m)
            if c == nk - 1 and u < n_sub - 1:
                pending = (acc, p, m, m_new)
            else:
                acc = accumulate(acc, p, m, m_new, c)
                m = m_new
    finish(acc, n_sub - 1)


def _attention(q, kt, v):
    b, _, l, _ = q.shape
    n_blk, t_blk = kt.shape[2], kt.shape[4]
    tk = TK_ATTN
    nk = l // tk
    tq = TQ_ATTN
    n_sub = SUBTILES_ATTN
    tq_step = n_sub * tq
    pairs_per_kv = N_Q_HEADS // N_KV_HEADS // 2
    return pl.pallas_call(
        functools.partial(_attn_kernel, nk=nk, tk=tk, tq=tq, n_sub=n_sub),
        grid=(b, N_Q_HEADS // 2, l // tq_step),
        in_specs=[
            pl.BlockSpec((1, 2, tq_step, HEAD_DIM), lambda bi, hp, i: (bi, hp, i, 0)),
            pl.BlockSpec((1, 1, n_blk, HEAD_DIM, t_blk), lambda bi, hp, i: (bi, hp // pairs_per_kv, 0, 0, 0)),
            pl.BlockSpec((1, 1, l, LANES), lambda bi, hp, i: (bi, hp // pairs_per_kv, 0, 0)),
        ],
        out_specs=pl.BlockSpec((1, tq_step, LANES), lambda bi, hp, i: (bi, i, hp)),
        out_shape=jax.ShapeDtypeStruct((b, l, ATTN_WIDTH), BF16),
        compiler_params=pltpu.CompilerParams(
            dimension_semantics=("parallel", "parallel", "parallel"), vmem_limit_bytes=VMEM_LIMIT),
        name="attn",
    )(q, kt, v)


def _ssd_kernel(xf_ref, dtf_ref, xb_ref, dtb_ref, alog_ref, dskip_ref, yf_ref, yb_ref,
                sf_sc, sb_sc, *, nck):
    q = SSD_CHUNK

    @pl.when(pl.program_id(1) == 0)
    def _():
        sf_sc[...] = jnp.zeros(sf_sc.shape, F32)
        sb_sc[...] = jnp.zeros(sb_sc.shape, F32)

    a16 = -jnp.exp(alog_ref[...]) * LOG2_E
    r = lax.broadcasted_iota(jnp.int32, (q, q), 0)
    c = lax.broadcasted_iota(jnp.int32, (q, q), 1)
    fwd_mask = c >= r
    bwd_mask = c <= r
    incl_fwd = jnp.where(fwd_mask, 1.0, 0.0).astype(BF16)
    incl_rev = jnp.where(bwd_mask, 1.0, 0.0).astype(BF16)
    row16 = lax.broadcasted_iota(jnp.int32, (2 * SSD_HEADS, q), 0)
    dsk = dskip_ref[...]

    def rows64(v8, h0):
        return jnp.concatenate(
            [jnp.broadcast_to(v8[h0 + k:h0 + k + 1], (SSD_HEAD_DIM, D_STATE)) for k in range(HEADS_PER_GROUP)],
            axis=0)

    sf = [sf_sc[g] for g in range(SSD_GROUPS)]
    sb = [sb_sc[g] for g in range(SSD_GROUPS)]

    def fwd_prep(j):
        cols = slice(j * q, (j + 1) * q)
        dt_t = dtf_ref[0, :, cols]
        ac = dt_t * a16
        cs = (_dot_exact(jnp.where(row16 < SSD_HEADS, ac, 0.0), incl_fwd)
              + _dot_exact(jnp.where(row16 >= SSD_HEADS, ac, 0.0), incl_rev))
        a_last = jnp.broadcast_to(cs[:SSD_HEADS, q - 1:q], (SSD_HEADS, q))
        exp_a = jnp.exp2(cs[:SSD_HEADS])
        wf = dt_t[:SSD_HEADS] * jnp.exp2(a_last - cs[:SSD_HEADS])
        dec = jnp.exp2(a_last)
        groups = []
        for g in range(SSD_GROUPS):
            bt = xf_ref[0, SSD_WIDTH + g * D_STATE:SSD_WIDTH + (g + 1) * D_STATE, cols]
            ct = xf_ref[0, SSD_WIDTH + (SSD_GROUPS + g) * D_STATE:
                        SSD_WIDTH + (SSD_GROUPS + g + 1) * D_STATE, cols]
            bg = bt.T.astype(BF16)
            ctb = ct.astype(BF16)
            sc_t = _dot(bg, ctb)
            groups.append((bg, ctb, jnp.where(fwd_mask, sc_t, 0.0), jnp.where(bwd_mask, sc_t, 0.0)))
        col_forms = [jnp.broadcast_to(cs[k:k + 1], (q, q)).T for k in range(2 * SSD_HEADS)]
        return cols, dt_t, cs, exp_a, wf, dec, groups, col_forms

    def fwd_main(prep):
        cols, dt_t, cs, exp_a, wf, dec, groups, col_forms = prep
        ys = []
        for g in range(SSD_GROUPS):
            bg, ctb, sc_fwd, sc_bwd = groups[g]
            st = sf[g]
            yoff = _dot(st.astype(BF16), ctb)
            xws = []
            for k in range(HEADS_PER_GROUP):
                hh = g * HEADS_PER_GROUP + k
                xh = xf_ref[0, hh * SSD_HEAD_DIM:(hh + 1) * SSD_HEAD_DIM, cols]
                arg = jnp.where(fwd_mask, cs[hh:hh + 1] - col_forms[hh],
                                cs[SSD_HEADS + hh:SSD_HEADS + hh + 1] - col_forms[SSD_HEADS + hh])
                e = jnp.exp2(arg)
                up = (e * sc_fwd).astype(BF16)
                lo = (e * sc_bwd).astype(BF16)
                lhs = jnp.concatenate([xh * dt_t[hh:hh + 1], xh * dt_t[SSD_HEADS + hh:SSD_HEADS + hh + 1]],
                                      axis=1).astype(BF16)
                yd = _dot(lhs, jnp.concatenate([up, lo], axis=0))
                ys.append(yd + yoff[k * SSD_HEAD_DIM:(k + 1) * SSD_HEAD_DIM] * exp_a[hh:hh + 1]
                          + xh * dsk[hh * SSD_HEAD_DIM:(hh + 1) * SSD_HEAD_DIM])
                xws.append(xh * wf[hh:hh + 1])
            xw = jnp.concatenate(xws, axis=0).astype(BF16)
            sf[g] = st * rows64(dec, g * HEADS_PER_GROUP) + _dot(xw, bg)
        yf_ref[0, :, cols] = jnp.concatenate(ys, axis=0)

    def bwd_prep(j):
        cols = slice(j * q, (j + 1) * q)
        dt_b = dtb_ref[0, SSD_HEADS:, cols]
        rcs = _dot_exact(dt_b * a16[SSD_HEADS:], incl_rev)
        r_first = jnp.broadcast_to(rcs[:, 0:1], (SSD_HEADS, q))
        exp_r = jnp.exp2(rcs)
        wb = dt_b * jnp.exp2(r_first - rcs)
        dec = jnp.exp2(r_first)
        groups = []
        for g in range(SSD_GROUPS):
            bt = xb_ref[0, SSD_WIDTH + g * D_STATE:SSD_WIDTH + (g + 1) * D_STATE, cols]
            ct = xb_ref[0, SSD_WIDTH + (SSD_GROUPS + g) * D_STATE:
                        SSD_WIDTH + (SSD_GROUPS + g + 1) * D_STATE, cols]
            groups.append((bt.T.astype(BF16), ct.astype(BF16)))
        return cols, exp_r, wb, dec, groups

    def bwd_main(prep):
        cols, exp_r, wb, dec, groups = prep
        ys = []
        for g in range(SSD_GROUPS):
            bg, ctb = groups[g]
            st = sb[g]
            yoff = _dot(st.astype(BF16), ctb)
            xws = []
            for k in range(HEADS_PER_GROUP):
                hh = g * HEADS_PER_GROUP + k
                xh = xb_ref[0, hh * SSD_HEAD_DIM:(hh + 1) * SSD_HEAD_DIM, cols]
                ys.append(yoff[k * SSD_HEAD_DIM:(k + 1) * SSD_HEAD_DIM] * exp_r[hh:hh + 1])
                xws.append(xh * wb[hh:hh + 1])
            xw = jnp.concatenate(xws, axis=0).astype(BF16)
            sb[g] = st * rows64(dec, g * HEADS_PER_GROUP) + _dot(xw, bg)
        yb_ref[0, :, cols] = jnp.concatenate(ys, axis=0)

    preps = [fwd_prep(j) for j in range(nck)]
    for j in range(nck):
        fwd_main(preps[j])
        bwd_main(bwd_prep(nck - 1 - j))

    for g in range(SSD_GROUPS):
        sf_sc[g] = sf[g]
        sb_sc[g] = sb[g]


def _ssd(xt, dtt, alog_col, dskip_col):
    b, c, l = xt.shape
    ts = TS_SSD
    ns = l // ts
    return pl.pallas_call(
        functools.partial(_ssd_kernel, nck=ts // SSD_CHUNK),
        grid=(b, ns),
        in_specs=[
            pl.BlockSpec((1, c, ts), lambda bi, i: (bi, 0, i)),
            pl.BlockSpec((1, 2 * SSD_HEADS, ts), lambda bi, i: (bi, 0, i)),
            pl.BlockSpec((1, c, ts), lambda bi, i: (bi, 0, ns - 1 - i)),
            pl.BlockSpec((1, 2 * SSD_HEADS, ts), lambda bi, i: (bi, 0, ns - 1 - i)),
            pl.BlockSpec((2 * SSD_HEADS, 1), lambda bi, i: (0, 0)),
            pl.BlockSpec((SSD_WIDTH, SSD_CHUNK), lambda bi, i: (0, 0)),
        ],
        out_specs=[
            pl.BlockSpec((1, SSD_WIDTH, ts), lambda bi, i: (bi, 0, i)),
            pl.BlockSpec((1, SSD_WIDTH, ts), lambda bi, i: (bi, 0, ns - 1 - i)),
        ],
        out_shape=[
            jax.ShapeDtypeStruct((b, SSD_WIDTH, l), F32),
            jax.ShapeDtypeStruct((b, SSD_WIDTH, l), F32),
        ],
        scratch_shapes=[pltpu.VMEM((SSD_GROUPS, GROUP_WIDTH, D_STATE), F32),
                        pltpu.VMEM((SSD_GROUPS, GROUP_WIDTH, D_STATE), F32)],
        compiler_params=pltpu.CompilerParams(
            dimension_semantics=("parallel", "arbitrary"), vmem_limit_bytes=VMEM_LIMIT),
        name="ssd",
    )(xt, dtt, xt, dtt, alog_col, dskip_col)


def _rms(x, w):
    ms = jnp.mean(x * x, axis=-1, keepdims=True)
    return x * lax.rsqrt(ms + EPS) * w


def _out_ffn_kernel(x_ref, attn_ref, yf_ref, yb_ref, z_ref, snw_ref, wo_ref, fnw_ref,
                    wg_ref, wu_ref, wd_ref, finw_ref, o_ref, *, final, d_ff, n_sub):
    ts = x_ref.shape[1] // n_sub

    def mix(u):
        rows = slice(u * ts, (u + 1) * ts)
        x = x_ref[0, rows]
        y = (yf_ref[0, :, rows] + yb_ref[0, :, rows]).T
        z = z_ref[0, rows]
        y = y * (z * _sigmoid(z))
        yn = jnp.concatenate(
            [_rms(y[:, g * GROUP_WIDTH:(g + 1) * GROUP_WIDTH], 1.0) for g in range(SSD_GROUPS)], axis=1)
        yn = (yn * snw_ref[...]).astype(BF16)
        mixed = jnp.concatenate([attn_ref[0, rows], yn], axis=1)
        x1 = x + _dot(mixed, wo_ref[...])
        return x1, _rms(x1, fnw_ref[...]).astype(BF16)

    mixes = [mix(u) for u in range(n_sub)]
    accs = [x1 for x1, _ in mixes]
    for c0, c1 in ((0, FF_SPLIT), (FF_SPLIT, d_ff)):
        for u in range(n_sub):
            h = mixes[u][1]
            gt = _dot(h, wg_ref[:, c0:c1])
            up = _dot(h, wu_ref[:, c0:c1])
            act = (gt * _sigmoid(gt) * up).astype(BF16)
            accs[u] = accs[u] + _dot(act, wd_ref[c0:c1, :])
    for u in range(n_sub):
        acc = accs[u]
        if final:
            acc = _rms(acc, finw_ref[...])
        o_ref[0, u * ts:(u + 1) * ts] = acc


def _out_ffn(x, attn, yf, yb, z, snw, wo, fnw, wg, wu, wd, finw, final):
    b, l, d = x.shape
    d_ff = wg.shape[1]
    tm = TM_FFN
    resident = lambda shape: pl.BlockSpec(shape, lambda bi, i: (0,) * len(shape),
                                          pipeline_mode=pl.Buffered(1))
    return pl.pallas_call(
        functools.partial(_out_ffn_kernel, final=final, d_ff=d_ff, n_sub=SUBTILES_FFN),
        grid=(b, l // tm),
        in_specs=[
            pl.BlockSpec((1, tm, d), lambda bi, i: (bi, i, 0)),
            pl.BlockSpec((1, tm, ATTN_WIDTH), lambda bi, i: (bi, i, 0)),
            pl.BlockSpec((1, SSD_WIDTH, tm), lambda bi, i: (bi, 0, i)),
            pl.BlockSpec((1, SSD_WIDTH, tm), lambda bi, i: (bi, 0, i)),
            pl.BlockSpec((1, tm, SSD_WIDTH), lambda bi, i: (bi, i, 0)),
            resident((1, SSD_WIDTH)),
            resident((ATTN_WIDTH + SSD_WIDTH, d)),
            resident((1, d)),
            resident((d, d_ff)),
            resident((d, d_ff)),
            resident((d_ff, d)),
            resident((1, d)),
        ],
        out_specs=pl.BlockSpec((1, tm, d), lambda bi, i: (bi, i, 0)),
        out_shape=jax.ShapeDtypeStruct((b, l, d), F32),
        compiler_params=pltpu.CompilerParams(
            dimension_semantics=("parallel", "parallel"), vmem_limit_bytes=VMEM_LIMIT),
        name="out_ffn",
    )(x, attn, yf, yb, z, snw, wo, fnw, wg, wu, wd, finw)


def _rope_tables(seq_len):
    t = jnp.arange(seq_len)
    row = (t // GRID_W).astype(F32)
    col = (t % GRID_W).astype(F32)
    half = ROPE_AXIS_DIM // 2
    inv_freq = ROPE_THETA ** (-(2.0 * jnp.arange(half, dtype=F32)) / ROPE_AXIS_DIM)
    ang_r = row[:, None] * inv_freq[None, :]
    ang_c = col[:, None] * inv_freq[None, :]
    cos = jnp.concatenate([jnp.cos(ang_r), jnp.cos(ang_r), jnp.cos(ang_c), jnp.cos(ang_c)], axis=1)
    sin = jnp.concatenate([-jnp.sin(ang_r), jnp.sin(ang_r), -jnp.sin(ang_c), jnp.sin(ang_c)], axis=1)
    reps = LANES // HEAD_DIM
    return jnp.tile(cos, (1, reps)), jnp.tile(sin, (1, reps))


def kernel(x, norm_mix_w, w_in, q_norm_w, k_norm_w, conv_w, conv_b, dt_bias, a_log, d_skip,
           ssd_norm_w, w_out, norm_ffn_w, w_gate, w_up, w_down, final_norm_w):
    b, l, d = x.shape
    depth = w_in.shape[0]
    assert l % (SUBTILES_MIX * TM_MIX) == 0 and l % TS_SSD == 0 and l % (SUBTILES_ATTN * TQ_ATTN) == 0 and l % TM_FFN == 0
    assert TM_MIX % TK_ATTN == 0
    assert l % GRID_W == 0 and w_in.shape[2] == OFF_DT + 2 * SSD_HEADS
    cos, sin = _rope_tables(l)
    finw = final_norm_w.reshape(1, d)
    for i in range(depth):
        w_in_pad = jnp.pad(w_in[i], ((0, 0), (0, D_IN_PAD - w_in.shape[2]))).astype(BF16)
        qkw = jnp.concatenate([jnp.tile(q_norm_w[i], N_Q_HEADS) * (LOG2_E / math.sqrt(HEAD_DIM)),
                               jnp.tile(k_norm_w[i], N_KV_HEADS)]).reshape(1, QK_WIDTH)
        dtb = jnp.pad(dt_bias[i].reshape(1, 2 * SSD_HEADS), ((0, 0), (0, LANES - 2 * SSD_HEADS)))
        q, kt, v, z, xt, dtt = _mix_in(x, norm_mix_w[i].reshape(1, d), w_in_pad, qkw, cos, sin, dtb,
                                       conv_w[i], conv_b[i].reshape(1, CONV_CH))
        attn = _attention(q, kt, v)
        yf, yb = _ssd(xt, dtt, a_log[i].reshape(2 * SSD_HEADS, 1),
                      jnp.broadcast_to(jnp.repeat(d_skip[i], SSD_HEAD_DIM)[:, None], (SSD_WIDTH, SSD_CHUNK)))
        x = _out_ffn(x, attn, yf, yb, z, ssd_norm_w[i].reshape(1, SSD_WIDTH), w_out[i].astype(BF16),
                     norm_ffn_w[i].reshape(1, d), w_gate[i].astype(BF16), w_up[i].astype(BF16),
                     w_down[i].astype(BF16), finw, final=(i == depth - 1))
    return x
```

```python
import functools
import math

import jax
import jax.numpy as jnp
from jax import lax
from jax.experimental import pallas as pl
from jax.experimental.pallas import tpu as pltpu

F32 = jnp.float32
BF16 = jnp.bfloat16

HEAD_DIM = 64
N_Q_HEADS = 8
N_KV_HEADS = 2
ATTN_WIDTH = N_Q_HEADS * HEAD_DIM
KV_WIDTH = N_KV_HEADS * HEAD_DIM
QK_WIDTH = ATTN_WIDTH + KV_WIDTH
SSD_HEADS = 8
SSD_HEAD_DIM = 64
SSD_WIDTH = SSD_HEADS * SSD_HEAD_DIM
SSD_GROUPS = 2
HEADS_PER_GROUP = SSD_HEADS // SSD_GROUPS
GROUP_WIDTH = SSD_WIDTH // SSD_GROUPS
D_STATE = 128
D_CONV = 5
CONV_PAD = D_CONV // 2
CONV_CH = SSD_WIDTH + 2 * SSD_GROUPS * D_STATE
GRID_W = 64
ROPE_THETA = 10000.0
ROPE_AXIS_DIM = HEAD_DIM // 2
EPS = 1e-6
LOG2_E = math.log2(math.e)

LANES = 128
SUBLANES = 8
VMEM_LIMIT = 56 * 1024 * 1024

OFF_QK = 0
OFF_V = QK_WIDTH
OFF_Z = OFF_V + KV_WIDTH
OFF_XBC = OFF_Z + SSD_WIDTH
OFF_DT = OFF_XBC + CONV_CH
D_IN_PAD = OFF_DT + LANES

TM_MIX = 512
SUBTILES_MIX = 2
TQ_ATTN = 512
SUBTILES_ATTN = 4
TK_ATTN = 512
SSD_CHUNK = 128
TS_SSD = 1024
TM_FFN = 512
SUBTILES_FFN = 2
FF_CHUNK = 1024

def _sigmoid(x):
    return 1.0 / (1.0 + jnp.exp(-x))


def _split3(a):
    hi = a.astype(BF16)
    r1 = a - hi.astype(F32)
    mid = r1.astype(BF16)
    lo = (r1 - mid.astype(F32)).astype(BF16)
    return hi, mid, lo


def _dot(a, b):
    return jnp.dot(a, b, preferred_element_type=F32)


def _dot_exact(a, m01):
    hi, mid, lo = _split3(a)
    return _dot(hi, m01) + _dot(mid, m01) + _dot(lo, m01)


def _mix_in_kernel(xp_ref, x_ref, xn_ref, nw_ref, w_ref, qkw_ref, cos_ref, sin_ref, dtb_ref, cw_ref, cb_ref,
                   q_ref, kt_ref, v_ref, z_ref, xt_ref, dtt_ref, *, tm, n_sub):
    i = pl.program_id(1)
    conv_tile = 2 * LANES
    cw = cw_ref[...]
    cb = cb_ref[...]
    qkw = qkw_ref[...]
    lane = lax.broadcasted_iota(jnp.int32, (tm, LANES), 1)
    low_head = lane < HEAD_DIM
    first_half = (lane & (ROPE_AXIS_DIM // 2)) == 0

    def project(u):
        before = xp_ref[0] if u == 0 else x_ref[0, u * tm - SUBLANES:u * tm]
        after = xn_ref[0] if u == n_sub - 1 else x_ref[0, (u + 1) * tm:(u + 1) * tm + SUBLANES]
        xe = jnp.concatenate([before, x_ref[0, u * tm:(u + 1) * tm], after], axis=0)
        ms = jnp.mean(xe * xe, axis=-1, keepdims=True)
        he = (xe * lax.rsqrt(ms + EPS) * nw_ref[...]).astype(BF16)
        h = he[SUBLANES:SUBLANES + tm]
        qkv = _dot(h, w_ref[:, OFF_QK:OFF_Z])
        xbc_tiles = [_dot(he, w_ref[:, OFF_XBC + c0:OFF_XBC + c0 + conv_tile])
                     for c0 in range(0, CONV_CH, conv_tile)]
        z_proj = _dot(h, w_ref[:, OFF_Z:OFF_XBC])
        dtp = _dot(h, w_ref[:, OFF_DT:D_IN_PAD]) + dtb_ref[...]
        return qkv, xbc_tiles, z_proj, dtp

    projections = [project(u) for u in range(n_sub)]
    for u in range(n_sub):
        _mix_in_epilogue(u, projections[u], i, cw, cb, qkw, lane, low_head, first_half, cos_ref, sin_ref,
                         q_ref, kt_ref, v_ref, z_ref, xt_ref, dtt_ref, tm=tm, n_sub=n_sub)


def _mix_in_epilogue(u, projection, i, cw, cb, qkw, lane, low_head, first_half, cos_ref, sin_ref,
                     q_ref, kt_ref, v_ref, z_ref, xt_ref, dtt_ref, *, tm, n_sub):
    qkv, xbc_tiles, z_proj, dtp = projection
    rows = slice(u * tm, (u + 1) * tm)
    n_ext = tm + 2 * SUBLANES
    conv_tile = 2 * LANES
    has_prev = (i > 0).astype(F32) if u == 0 else None
    has_next = (i < pl.num_programs(1) - 1).astype(F32) if u == n_sub - 1 else None

    def conv_epilogue(t):
        ch = slice(t * conv_tile, (t + 1) * conv_tile)
        xbc = xbc_tiles[t]
        head = xbc[:SUBLANES] if has_prev is None else xbc[:SUBLANES] * has_prev
        tail = xbc[SUBLANES + tm:] if has_next is None else xbc[SUBLANES + tm:] * has_next
        xbc = jnp.concatenate([head, xbc[SUBLANES:SUBLANES + tm], tail], axis=0)
        acc = jnp.broadcast_to(cb[:, ch], (tm, conv_tile))
        for k in range(D_CONV):
            shift = (CONV_PAD - k) % n_ext
            rolled = xbc if shift == 0 else pltpu.roll(xbc, shift, 0)
            acc = acc + rolled[SUBLANES:SUBLANES + tm] * cw[k:k + 1, ch]
        xt_ref[0, ch, rows] = (acc * _sigmoid(acc)).T

    cos = cos_ref[rows, :]
    sin = sin_ref[rows, :]

    def qk_epilogue(c):
        xc = qkv[:, c * LANES:(c + 1) * LANES]
        sq = xc * xc
        ss_lo = jnp.sum(jnp.where(low_head, sq, 0.0), axis=-1, keepdims=True)
        ss_hi = jnp.sum(jnp.where(low_head, 0.0, sq), axis=-1, keepdims=True)
        inv = jnp.where(low_head, lax.rsqrt(ss_lo * (1.0 / HEAD_DIM) + EPS),
                        lax.rsqrt(ss_hi * (1.0 / HEAD_DIM) + EPS))
        xn = xc * inv * qkw[:, c * LANES:(c + 1) * LANES]
        partner = jnp.where(first_half,
                            pltpu.roll(xn, LANES - ROPE_AXIS_DIM // 2, 1),
                            pltpu.roll(xn, ROPE_AXIS_DIM // 2, 1))
        rot = xn * cos + partner * sin
        if c < ATTN_WIDTH // LANES:
            q_ref[0, 2 * c, rows] = rot[:, :HEAD_DIM].astype(BF16)
            q_ref[0, 2 * c + 1, rows] = rot[:, HEAD_DIM:].astype(BF16)
        else:
            kt = rot.T
            kt_ref[0, 0, u] = kt[:HEAD_DIM].astype(BF16)
            kt_ref[0, 1, u] = kt[HEAD_DIM:].astype(BF16)

    def v_epilogue():
        vv = qkv[:, OFF_V:OFF_V + KV_WIDTH]
        ones_col = jnp.where(lane == HEAD_DIM, 1.0, 0.0)
        v_ref[0, 0, rows] = jnp.where(low_head, vv, ones_col).astype(BF16)
        v_ref[0, 1, rows] = jnp.where(low_head, pltpu.roll(vv, HEAD_DIM, 1), ones_col).astype(BF16)

    for c in range(QK_WIDTH // LANES):
        qk_epilogue(c)
    v_epilogue()
    for t in range(CONV_CH // conv_tile):
        conv_epilogue(t)

    z_ref[0, rows] = z_proj
    dt = jnp.maximum(dtp, 0.0) + jnp.log1p(jnp.exp(-jnp.abs(dtp)))
    dtt_ref[0, :, rows] = dt.T[:2 * SSD_HEADS]


def _mix_in(x, nw, w_in_pad, qkw, cos, sin, dtb, conv_w, conv_b):
    b, l, d = x.shape
    tm = TM_MIX
    n_sub = SUBTILES_MIX
    ts = n_sub * tm
    nt = l // tm
    rb = ts // SUBLANES
    last = l // SUBLANES - 1
    const = lambda shape: pl.BlockSpec(shape, lambda bi, i: (0,) * len(shape))
    return pl.pallas_call(
        functools.partial(_mix_in_kernel, tm=tm, n_sub=n_sub),
        grid=(b, l // ts),
        in_specs=[
            pl.BlockSpec((1, SUBLANES, d), lambda bi, i: (bi, jnp.maximum(i * rb - 1, 0), 0)),
            pl.BlockSpec((1, ts, d), lambda bi, i: (bi, i, 0)),
            pl.BlockSpec((1, SUBLANES, d), lambda bi, i: (bi, jnp.minimum((i + 1) * rb, last), 0)),
            const((1, d)),
            const((d, D_IN_PAD)),
            const((1, QK_WIDTH)),
            pl.BlockSpec((ts, LANES), lambda bi, i: (i, 0)),
            pl.BlockSpec((ts, LANES), lambda bi, i: (i, 0)),
            const((1, LANES)),
            const((D_CONV, CONV_CH)),
            const((1, CONV_CH)),
        ],
        out_specs=[
            pl.BlockSpec((1, N_Q_HEADS, ts, HEAD_DIM), lambda bi, i: (bi, 0, i, 0)),
            pl.BlockSpec((1, N_KV_HEADS, n_sub, HEAD_DIM, tm), lambda bi, i: (bi, 0, i, 0, 0)),
            pl.BlockSpec((1, N_KV_HEADS, ts, LANES), lambda bi, i: (bi, 0, i, 0)),
            pl.BlockSpec((1, ts, SSD_WIDTH), lambda bi, i: (bi, i, 0)),
            pl.BlockSpec((1, CONV_CH, ts), lambda bi, i: (bi, 0, i)),
            pl.BlockSpec((1, 2 * SSD_HEADS, ts), lambda bi, i: (bi, 0, i)),
        ],
        out_shape=[
            jax.ShapeDtypeStruct((b, N_Q_HEADS, l, HEAD_DIM), BF16),
            jax.ShapeDtypeStruct((b, N_KV_HEADS, nt, HEAD_DIM, tm), BF16),
            jax.ShapeDtypeStruct((b, N_KV_HEADS, l, LANES), BF16),
            jax.ShapeDtypeStruct((b, l, SSD_WIDTH), F32),
            jax.ShapeDtypeStruct((b, CONV_CH, l), F32),
            jax.ShapeDtypeStruct((b, 2 * SSD_HEADS, l), F32),
        ],
        compiler_params=pltpu.CompilerParams(
            dimension_semantics=("parallel", "parallel"), vmem_limit_bytes=VMEM_LIMIT),
        name="mix_in",
    )(x, x, x, nw, w_in_pad, qkw, cos, sin, dtb, conv_w, conv_b)


def _attn_kernel(q_ref, kt_ref, v_ref, o_ref, *, nk, tk, tq, n_sub):
    per_block = kt_ref.shape[4] // tk
    lane = lax.broadcasted_iota(jnp.int32, (tq, LANES), 1)

    def scores(q, c):
        kt = kt_ref[0, 0, c // per_block, :, (c % per_block) * tk:(c % per_block + 1) * tk]
        return _dot(q, kt)

    def softmax(s, m):
        m_new = jnp.maximum(m, jnp.max(s, axis=-1, keepdims=True))
        p = jnp.exp2((s - jnp.concatenate([m_new] * (tk // LANES), axis=1)).astype(BF16))
        return p, m_new

    def accumulate(acc, p, m, m_new, c):
        return jnp.exp2(m - m_new) * acc + _dot(p, v_ref[0, 0, c * tk:(c + 1) * tk, :])

    def finish(acc, u):
        o = acc * (1.0 / acc[:, HEAD_DIM:HEAD_DIM + 1])
        o_ref[0, u * tq:(u + 1) * tq, :] = jnp.where(
            lane < HEAD_DIM, o[:tq], pltpu.roll(o[tq:], HEAD_DIM, 1)).astype(BF16)

    pending = None
    for u in range(n_sub):
        q = q_ref[0, :, u * tq:(u + 1) * tq, :].reshape(2 * tq, HEAD_DIM)
        m = jnp.full((2 * tq, LANES), -jnp.inf, F32)
        acc = jnp.zeros((2 * tq, LANES), F32)
        s = scores(q, 0)
        if pending is not None:
            p_acc, p_p, p_m, p_mn = pending
            p0, m0 = softmax(s, m)
            finish(accumulate(p_acc, p_p, p_m, p_mn, nk - 1), u - 1)
            acc = accumulate(acc, p0, m, m0, 0)
            m = m0
            first = 1
        else:
            first = 0
        for c in range(first, nk):
            if c > 0:
                s = scores(q, c)
            p, m_new = softmax(s, m)
            if c == nk - 1 and u < n_sub - 1:
                pending = (acc, p, m, m_new)
            else:
                acc = accumulate(acc, p, m, m_new, c)
                m = m_new
    finish(acc, n_sub - 1)


def _attention(q, kt, v):
    b, _, l, _ = q.shape
    n_blk, t_blk = kt.shape[2], kt.shape[4]
    tk = TK_ATTN
    nk = l // tk
    tq = TQ_ATTN
    n_sub = SUBTILES_ATTN
    tq_step = n_sub * tq
    pairs_per_kv = N_Q_HEADS // N_KV_HEADS // 2
    return pl.pallas_call(
        functools.partial(_attn_kernel, nk=nk, tk=tk, tq=tq, n_sub=n_sub),
        grid=(b, N_Q_HEADS // 2, l // tq_step),
        in_specs=[
            pl.BlockSpec((1, 2, tq_step, HEAD_DIM), lambda bi, hp, i: (bi, hp, i, 0)),
            pl.BlockSpec((1, 1, n_blk, HEAD_DIM, t_blk), lambda bi, hp, i: (bi, hp // pairs_per_kv, 0, 0, 0)),
            pl.BlockSpec((1, 1, l, LANES), lambda bi, hp, i: (bi, hp // pairs_per_kv, 0, 0)),
        ],
        out_specs=pl.BlockSpec((1, tq_step, LANES), lambda bi, hp, i: (bi, i, hp)),
        out_shape=jax.ShapeDtypeStruct((b, l, ATTN_WIDTH), BF16),
        compiler_params=pltpu.CompilerParams(
            dimension_semantics=("parallel", "parallel", "parallel"), vmem_limit_bytes=VMEM_LIMIT),
        name="attn",
    )(q, kt, v)


def _ssd_kernel(xf_ref, dtf_ref, xb_ref, dtb_ref, alog_ref, dskip_ref, yf_ref, yb_ref,
                sf_sc, sb_sc, *, nck):
    q = SSD_CHUNK

    @pl.when(pl.program_id(1) == 0)
    def _():
        sf_sc[...] = jnp.zeros(sf_sc.shape, F32)
        sb_sc[...] = jnp.zeros(sb_sc.shape, F32)

    a16 = -jnp.exp(alog_ref[...]) * LOG2_E
    r = lax.broadcasted_iota(jnp.int32, (q, q), 0)
    c = lax.broadcasted_iota(jnp.int32, (q, q), 1)
    fwd_mask = c >= r
    bwd_mask = c <= r
    incl_fwd = jnp.where(fwd_mask, 1.0, 0.0).astype(BF16)
    incl_rev = jnp.where(bwd_mask, 1.0, 0.0).astype(BF16)
    row16 = lax.broadcasted_iota(jnp.int32, (2 * SSD_HEADS, q), 0)
    dsk = dskip_ref[...]

    def rows64(v8, h0):
        return jnp.concatenate(
            [jnp.broadcast_to(v8[h0 + k:h0 + k + 1], (SSD_HEAD_DIM, D_STATE)) for k in range(HEADS_PER_GROUP)],
            axis=0)

    sf = [sf_sc[g] for g in range(SSD_GROUPS)]
    sb = [sb_sc[g] for g in range(SSD_GROUPS)]

    def fwd_prep(j):
        cols = slice(j * q, (j + 1) * q)
        dt_t = dtf_ref[0, :, cols]
        ac = dt_t * a16
        cs = (_dot_exact(jnp.where(row16 < SSD_HEADS, ac, 0.0), incl_fwd)
              + _dot_exact(jnp.where(row16 >= SSD_HEADS, ac, 0.0), incl_rev))
        a_last = jnp.broadcast_to(cs[:SSD_HEADS, q - 1:q], (SSD_HEADS, q))
        exp_a = jnp.exp2(cs[:SSD_HEADS])
        wf = dt_t[:SSD_HEADS] * jnp.exp2(a_last - cs[:SSD_HEADS])
        dec = jnp.exp2(a_last)
        groups = []
        for g in range(SSD_GROUPS):
            bt = xf_ref[0, SSD_WIDTH + g * D_STATE:SSD_WIDTH + (g + 1) * D_STATE, cols]
            ct = xf_ref[0, SSD_WIDTH + (SSD_GROUPS + g) * D_STATE:
                        SSD_WIDTH + (SSD_GROUPS + g + 1) * D_STATE, cols]
            bg = bt.T.astype(BF16)
            ctb = ct.astype(BF16)
            sc_t = _dot(bg, ctb)
            groups.append((bg, ctb, jnp.where(fwd_mask, sc_t, 0.0), jnp.where(bwd_mask, sc_t, 0.0)))
        col_forms = [jnp.broadcast_to(cs[k:k + 1], (q, q)).T for k in range(2 * SSD_HEADS)]
        return cols, dt_t, cs, exp_a, wf, dec, groups, col_forms

    def fwd_main(prep):
        cols, dt_t, cs, exp_a, wf, dec, groups, col_forms = prep
        ys = []
        for g in range(SSD_GROUPS):
            bg, ctb, sc_fwd, sc_bwd = groups[g]
            st = sf[g]
            yoff = _dot(st.astype(BF16), ctb)
            xws = []
            for k in range(HEADS_PER_GROUP):
                hh = g * HEADS_PER_GROUP + k
                xh = xf_ref[0, hh * SSD_HEAD_DIM:(hh + 1) * SSD_HEAD_DIM, cols]
                arg = jnp.where(fwd_mask, cs[hh:hh + 1] - col_forms[hh],
                                cs[SSD_HEADS + hh:SSD_HEADS + hh + 1] - col_forms[SSD_HEADS + hh])
                e = jnp.exp2(arg)
                up = (e * sc_fwd).astype(BF16)
                lo = (e * sc_bwd).astype(BF16)
                lhs = jnp.concatenate([xh * dt_t[hh:hh + 1], xh * dt_t[SSD_HEADS + hh:SSD_HEADS + hh + 1]],
                                      axis=1).astype(BF16)
                yd = _dot(lhs, jnp.concatenate([up, lo], axis=0))
                ys.append(yd + yoff[k * SSD_HEAD_DIM:(k + 1) * SSD_HEAD_DIM] * exp_a[hh:hh + 1]
                          + xh * dsk[hh * SSD_HEAD_DIM:(hh + 1) * SSD_HEAD_DIM])
                xws.append(xh * wf[hh:hh + 1])
            xw = jnp.concatenate(xws, axis=0).astype(BF16)
            sf[g] = st * rows64(dec, g * HEADS_PER_GROUP) + _dot(xw, bg)
        yf_ref[0, :, cols] = jnp.concatenate(ys, axis=0)

    def bwd_prep(j):
        cols = slice(j * q, (j + 1) * q)
        dt_b = dtb_ref[0, SSD_HEADS:, cols]
        rcs = _dot_exact(dt_b * a16[SSD_HEADS:], incl_rev)
        r_first = jnp.broadcast_to(rcs[:, 0:1], (SSD_HEADS, q))
        exp_r = jnp.exp2(rcs)
        wb = dt_b * jnp.exp2(r_first - rcs)
        dec = jnp.exp2(r_first)
        groups = []
        for g in range(SSD_GROUPS):
            bt = xb_ref[0, SSD_WIDTH + g * D_STATE:SSD_WIDTH + (g + 1) * D_STATE, cols]
            ct = xb_ref[0, SSD_WIDTH + (SSD_GROUPS + g) * D_STATE:
                        SSD_WIDTH + (SSD_GROUPS + g + 1) * D_STATE, cols]
            groups.append((bt.T.astype(BF16), ct.astype(BF16)))
        return cols, exp_r, wb, dec, groups

    def bwd_main(prep):
        cols, exp_r, wb, dec, groups = prep
        ys = []
        for g in range(SSD_GROUPS):
            bg, ctb = groups[g]
            st = sb[g]
            yoff = _dot(st.astype(BF16), ctb)
            xws = []
            for k in range(HEADS_PER_GROUP):
                hh = g * HEADS_PER_GROUP + k
                xh = xb_ref[0, hh * SSD_HEAD_DIM:(hh + 1) * SSD_HEAD_DIM, cols]
                ys.append(yoff[k * SSD_HEAD_DIM:(k + 1) * SSD_HEAD_DIM] * exp_r[hh:hh + 1])
                xws.append(xh * wb[hh:hh + 1])
            xw = jnp.concatenate(xws, axis=0).astype(BF16)
            sb[g] = st * rows64(dec, g * HEADS_PER_GROUP) + _dot(xw, bg)
        yb_ref[0, :, cols] = jnp.concatenate(ys, axis=0)

    preps = [fwd_prep(j) for j in range(nck)]
    for j in range(nck):
        fwd_main(preps[j])
        bwd_main(bwd_prep(nck - 1 - j))

    for g in range(SSD_GROUPS):
        sf_sc[g] = sf[g]
        sb_sc[g] = sb[g]


def _ssd(xt, dtt, alog_col, dskip_col):
    b, c, l = xt.shape
    ts = TS_SSD
    ns = l // ts
    return pl.pallas_call(
        functools.partial(_ssd_kernel, nck=ts // SSD_CHUNK),
        grid=(b, ns),
        in_specs=[
            pl.BlockSpec((1, c, ts), lambda bi, i: (bi, 0, i)),
            pl.BlockSpec((1, 2 * SSD_HEADS, ts), lambda bi, i: (bi, 0, i)),
            pl.BlockSpec((1, c, ts), lambda bi, i: (bi, 0, ns - 1 - i)),
            pl.BlockSpec((1, 2 * SSD_HEADS, ts), lambda bi, i: (bi, 0, ns - 1 - i)),
            pl.BlockSpec((2 * SSD_HEADS, 1), lambda bi, i: (0, 0)),
            pl.BlockSpec((SSD_WIDTH, SSD_CHUNK), lambda bi, i: (0, 0)),
        ],
        out_specs=[
            pl.BlockSpec((1, SSD_WIDTH, ts), lambda bi, i: (bi, 0, i)),
            pl.BlockSpec((1, SSD_WIDTH, ts), lambda bi, i: (bi, 0, ns - 1 - i)),
        ],
        out_shape=[
            jax.ShapeDtypeStruct((b, SSD_WIDTH, l), F32),
            jax.ShapeDtypeStruct((b, SSD_WIDTH, l), F32),
        ],
        scratch_shapes=[pltpu.VMEM((SSD_GROUPS, GROUP_WIDTH, D_STATE), F32),
                        pltpu.VMEM((SSD_GROUPS, GROUP_WIDTH, D_STATE), F32)],
        compiler_params=pltpu.CompilerParams(
            dimension_semantics=("parallel", "arbitrary"), vmem_limit_bytes=VMEM_LIMIT),
        name="ssd",
    )(xt, dtt, xt, dtt, alog_col, dskip_col)


def _rms(x, w):
    ms = jnp.mean(x * x, axis=-1, keepdims=True)
    return x * lax.rsqrt(ms + EPS) * w


def _out_ffn_kernel(x_ref, attn_ref, yf_ref, yb_ref, z_ref, snw_ref, wo_ref, fnw_ref,
                    wg_ref, wu_ref, wd_ref, finw_ref, o_ref, *, final, d_ff, n_sub):
    ts = x_ref.shape[1] // n_sub

    def mix(u):
        rows = slice(u * ts, (u + 1) * ts)
        x = x_ref[0, rows]
        y = (yf_ref[0, :, rows] + yb_ref[0, :, rows]).T
        z = z_ref[0, rows]
        y = y * (z * _sigmoid(z))
        yn = jnp.concatenate(
            [_rms(y[:, g * GROUP_WIDTH:(g + 1) * GROUP_WIDTH], 1.0) for g in range(SSD_GROUPS)], axis=1)
        yn = (yn * snw_ref[...]).astype(BF16)
        mixed = jnp.concatenate([attn_ref[0, rows], yn], axis=1)
        x1 = x + _dot(mixed, wo_ref[...])
        return x1, _rms(x1, fnw_ref[...]).astype(BF16)

    mixes = [mix(u) for u in range(n_sub)]
    accs = [x1 for x1, _ in mixes]
    for c0 in range(0, d_ff, FF_CHUNK):
        c1 = min(c0 + FF_CHUNK, d_ff)
        for u in range(n_sub):
            h = mixes[u][1]
            gt = _dot(h, wg_ref[:, c0:c1])
            up = _dot(h, wu_ref[:, c0:c1])
            act = (gt * _sigmoid(gt) * up).astype(BF16)
            accs[u] = accs[u] + _dot(act, wd_ref[c0:c1, :])
    for u in range(n_sub):
        acc = accs[u]
        if final:
            acc = _rms(acc, finw_ref[...])
        o_ref[0, u * ts:(u + 1) * ts] = acc


def _out_ffn(x, attn, yf, yb, z, snw, wo, fnw, wg, wu, wd, finw, final):
    b, l, d = x.shape
    d_ff = wg.shape[1]
    tm = TM_FFN
    resident = lambda shape: pl.BlockSpec(shape, lambda bi, i: (0,) * len(shape),
                                          pipeline_mode=pl.Buffered(1))
    return pl.pallas_call(
        functools.partial(_out_ffn_kernel, final=final, d_ff=d_ff, n_sub=SUBTILES_FFN),
        grid=(b, l // tm),
        in_specs=[
            pl.BlockSpec((1, tm, d), lambda bi, i: (bi, i, 0)),
            pl.BlockSpec((1, tm, ATTN_WIDTH), lambda bi, i: (bi, i, 0)),
            pl.BlockSpec((1, SSD_WIDTH, tm), lambda bi, i: (bi, 0, i)),
            pl.BlockSpec((1, SSD_WIDTH, tm), lambda bi, i: (bi, 0, i)),
            pl.BlockSpec((1, tm, SSD_WIDTH), lambda bi, i: (bi, i, 0)),
            resident((1, SSD_WIDTH)),
            resident((ATTN_WIDTH + SSD_WIDTH, d)),
            resident((1, d)),
            resident((d, d_ff)),
            resident((d, d_ff)),
            resident((d_ff, d)),
            resident((1, d)),
        ],
        out_specs=pl.BlockSpec((1, tm, d), lambda bi, i: (bi, i, 0)),
        out_shape=jax.ShapeDtypeStruct((b, l, d), F32),
        compiler_params=pltpu.CompilerParams(
            dimension_semantics=("parallel", "parallel"), vmem_limit_bytes=VMEM_LIMIT),
        name="out_ffn",
    )(x, attn, yf, yb, z, snw, wo, fnw, wg, wu, wd, finw)


def _rope_tables(seq_len):
    t = jnp.arange(seq_len)
    row = (t // GRID_W).astype(F32)
    col = (t % GRID_W).astype(F32)
    half = ROPE_AXIS_DIM // 2
    inv_freq = ROPE_THETA ** (-(2.0 * jnp.arange(half, dtype=F32)) / ROPE_AXIS_DIM)
    ang_r = row[:, None] * inv_freq[None, :]
    ang_c = col[:, None] * inv_freq[None, :]
    cos = jnp.concatenate([jnp.cos(ang_r), jnp.cos(ang_r), jnp.cos(ang_c), jnp.cos(ang_c)], axis=1)
    sin = jnp.concatenate([-jnp.sin(ang_r), jnp.sin(ang_r), -jnp.sin(ang_c), jnp.sin(ang_c)], axis=1)
    reps = LANES // HEAD_DIM
    return jnp.tile(cos, (1, reps)), jnp.tile(sin, (1, reps))


def kernel(x, norm_mix_w, w_in, q_norm_w, k_norm_w, conv_w, conv_b, dt_bias, a_log, d_skip,
           ssd_norm_w, w_out, norm_ffn_w, w_gate, w_up, w_down, final_norm_w):
    b, l, d = x.shape
    depth = w_in.shape[0]
    assert l % (SUBTILES_MIX * TM_MIX) == 0 and l % TS_SSD == 0 and l % (SUBTILES_ATTN * TQ_ATTN) == 0 and l % TM_FFN == 0
    assert TM_MIX % TK_ATTN == 0
    assert l % GRID_W == 0 and w_in.shape[2] == OFF_DT + 2 * SSD_HEADS
    cos, sin = _rope_tables(l)
    finw = final_norm_w.reshape(1, d)
    for i in range(depth):
        w_in_pad = jnp.pad(w_in[i], ((0, 0), (0, D_IN_PAD - w_in.shape[2]))).astype(BF16)
        qkw = jnp.concatenate([jnp.tile(q_norm_w[i], N_Q_HEADS) * (LOG2_E / math.sqrt(HEAD_DIM)),
                               jnp.tile(k_norm_w[i], N_KV_HEADS)]).reshape(1, QK_WIDTH)
        dtb = jnp.pad(dt_bias[i].reshape(1, 2 * SSD_HEADS), ((0, 0), (0, LANES - 2 * SSD_HEADS)))
        q, kt, v, z, xt, dtt = _mix_in(x, norm_mix_w[i].reshape(1, d), w_in_pad, qkw, cos, sin, dtb,
                                       conv_w[i], conv_b[i].reshape(1, CONV_CH))
        attn = _attention(q, kt, v)
        yf, yb = _ssd(xt, dtt, a_log[i].reshape(2 * SSD_HEADS, 1),
                      jnp.broadcast_to(jnp.repeat(d_skip[i], SSD_HEAD_DIM)[:, None], (SSD_WIDTH, SSD_CHUNK)))
        x = _out_ffn(x, attn, yf, yb, z, ssd_norm_w[i].reshape(1, SSD_WIDTH), w_out[i].astype(BF16),
                     norm_ffn_w[i].reshape(1, d), w_gate[i].astype(BF16), w_up[i].astype(BF16),
                     w_down[i].astype(BF16), finw, final=(i == depth - 1))
    return x
```
